```python
import math
import jax, jax.numpy as jnp
from jax import lax
import numpy as np

D_MODEL = 1024
BATCH = 16
SEQ = 2048
DEPTH = 4

D_FF = 2816
N_BRANCH = 3
BRANCH_WIDTH = D_MODEL // 2
POOL_WINDOWS = (2, 4, 8, 16)
POOL_GROUPS = len(POOL_WINDOWS)
POOL_GROUP_DIM = BRANCH_WIDTH // POOL_GROUPS
DN_HEAD_DIM = 128
DN_HEADS = BRANCH_WIDTH // DN_HEAD_DIM
DN_CONV = 4
DN_CHUNK = 64
SB_HEAD_DIM = 128
SB_HEADS = BRANCH_WIDTH // SB_HEAD_DIM
SB_BLOCK = 128
EPS = 1e-6

IN_SPLITS = (
    BRANCH_WIDTH,
    3 * BRANCH_WIDTH,
    BRANCH_WIDTH,
    DN_HEADS,
    DN_HEADS,
    3 * BRANCH_WIDTH,
    N_BRANCH * D_MODEL,
)
P_IN = sum(IN_SPLITS)

kernel_name = "hybrid_pool_deltanet_stickbreak_macaron"


def rms_norm(x, g):
    xf = x.astype(jnp.float32)
    y = xf * lax.rsqrt(jnp.mean(xf * xf, axis=-1, keepdims=True) + EPS)
    return (y * g.astype(jnp.float32)).astype(x.dtype)


def swiglu(h, w_gate, w_up, w_down):
    return (jax.nn.silu(h @ w_gate) * (h @ w_up)) @ w_down


def pool_mixer(u, w_group, scale):
    b, s, _ = u.shape
    uf = u.astype(jnp.float32)
    csum = jnp.cumsum(uf, axis=1)
    count = jnp.arange(1, s + 1, dtype=jnp.float32)[None, :, None]
    outs = []
    for gi, win in enumerate(POOL_WINDOWS):
        sl = slice(gi * POOL_GROUP_DIM, (gi + 1) * POOL_GROUP_DIM)
        c = csum[..., sl]
        c_lag = jnp.pad(c, ((0, 0), (win, 0), (0, 0)))[:, :s]
        mean = (c - c_lag) / jnp.minimum(count, float(win))
        outs.append(mean - uf[..., sl])
    pooled = jnp.stack(outs, axis=2).astype(u.dtype)
    mixed = jnp.einsum('bsgc,gcd->bsgd', pooled, w_group).reshape(b, s, BRANCH_WIDTH)
    return mixed * scale


def causal_depthwise_conv(x, w):
    k, c = w.shape
    return lax.conv_general_dilated(
        x, w[:, None, :].astype(x.dtype), window_strides=(1,), padding=((k - 1, 0),),
        dimension_numbers=('NWC', 'WIO', 'NWC'), feature_group_count=c)


def gated_deltanet(qkv_in, z, a, b_logit, conv_w, A_log, dt_bias, out_gain):
    f32 = jnp.float32
    bsz, s, _ = qkv_in.shape
    h, d, c = DN_HEADS, DN_HEAD_DIM, DN_CHUNK
    n = s // c
    qkv = jax.nn.silu(causal_depthwise_conv(qkv_in, conv_w)).astype(f32)
    q, k, v = jnp.split(qkv, 3, axis=-1)

    def to_chunks(t):
        return t.reshape(bsz, n, c, h, d).transpose(0, 3, 1, 2, 4)

    q, k, v = to_chunks(q), to_chunks(k), to_chunks(v)
    q = q * lax.rsqrt(jnp.sum(q * q, -1, keepdims=True) + EPS) * (d ** -0.5)
    k = k * lax.rsqrt(jnp.sum(k * k, -1, keepdims=True) + EPS)
    beta = jax.nn.sigmoid(b_logit.astype(f32)).reshape(bsz, n, c, h).transpose(0, 3, 1, 2)
    g = -jnp.exp(A_log.astype(f32)) * jax.nn.softplus(a.astype(f32) + dt_bias.astype(f32))
    g = g.reshape(bsz, n, c, h).transpose(0, 3, 1, 2)
    gc = jnp.cumsum(g, axis=-1)

    idx = jnp.arange(c)
    lower_incl = idx[:, None] >= idx[None, :]
    strict = idx[:, None] > idx[None, :]
    diff = gc[..., :, None] - gc[..., None, :]
    decay = jnp.where(lower_incl, jnp.exp(jnp.where(lower_incl, diff, 0.0)), 0.0)

    kb = k * beta[..., None]
    lmat = jnp.einsum('bhnid,bhnjd->bhnij', kb, k) * jnp.where(strict, decay, 0.0)
    eye = jnp.eye(c, dtype=f32)
    rhs = jnp.concatenate([v * beta[..., None], kb * jnp.exp(gc)[..., None]], axis=-1)
    sol = lax.linalg.triangular_solve(lmat + eye, rhs, left_side=True, lower=True, unit_diagonal=True)
    u, w = sol[..., :d], sol[..., d:]

    attn_qk = jnp.einsum('bhnid,bhnjd->bhnij', q, k) * decay
    q_dec = q * jnp.exp(gc)[..., None]
    k_dec = k * jnp.exp(gc[..., -1:] - gc)[..., None]
    chunk_decay = jnp.exp(gc[..., -1])

    def step(state, xs):
        u_n, w_n, qd_n, kd_n, a_n, cd_n = xs
        v_new = u_n - jnp.einsum('bhcd,bhde->bhce', w_n, state)
        o_n = (jnp.einsum('bhcd,bhde->bhce', qd_n, state)
               + jnp.einsum('bhij,bhje->bhie', a_n, v_new))
        state = state * cd_n[..., None, None] + jnp.einsum('bhcd,bhce->bhde', kd_n, v_new)
        return state, o_n

    xs = tuple(jnp.moveaxis(t, 2, 0) for t in (u, w, q_dec, k_dec, attn_qk, chunk_decay))
    state0 = jnp.zeros((bsz, h, d, d), f32)
    _, o = lax.scan(step, state0, xs)
    o = o.transpose(1, 0, 3, 2, 4).reshape(bsz, s, h, d)
    o = o * lax.rsqrt(jnp.mean(o * o, -1, keepdims=True) + EPS) * out_gain.astype(f32)
    o = o * jax.nn.silu(z.astype(f32)).reshape(bsz, s, h, d)
    return o.reshape(bsz, s, BRANCH_WIDTH).astype(qkv_in.dtype)


def stick_breaking_attention(qkv):
    f32 = jnp.float32
    bsz, s, _ = qkv.shape
    h, d, blk = SB_HEADS, SB_HEAD_DIM, SB_BLOCK
    q, k, v = [t.reshape(bsz, s, h, d).transpose(0, 2, 1, 3) for t in jnp.split(qkv, 3, axis=-1)]
    scale = d ** -0.5
    outs = []
    for i in range(s // blk):
        q0, kl = i * blk, (i + 1) * blk
        qb = q[:, :, q0:kl]
        kb, vb = k[:, :, :kl], v[:, :, :kl]
        logits = jnp.einsum('bhqd,bhkd->bhqk', qb, kb).astype(f32) * scale
        causal = jnp.arange(kl)[None, :] < jnp.arange(q0, kl)[:, None]
        log_not = jnp.where(causal, jax.nn.log_sigmoid(-logits), 0.0)
        tail = lax.cumsum(log_not, axis=3, reverse=True) - log_not
        weights = jnp.where(causal, jnp.exp(jax.nn.log_sigmoid(logits) + tail), 0.0)
        outs.append(jnp.einsum('bhqk,bhkd->bhqd', weights.astype(vb.dtype), vb))
    o = jnp.concatenate(outs, axis=2)
    return o.transpose(0, 2, 1, 3).reshape(bsz, s, BRANCH_WIDTH)


def setup_inputs(seed: int = 0) -> dict:
    key = jax.random.key(seed)
    ks = jax.random.split(key, 17)
    L, D, F = DEPTH, D_MODEL, D_FF
    f32 = jnp.float32

    def dense(k, shape, fan_in):
        return jax.random.normal(k, shape, f32) * (fan_in ** -0.5)

    def gain(k, shape):
        return 1.0 + 0.02 * jax.random.normal(k, shape, f32)

    dt = jnp.exp(jax.random.uniform(ks[12], (L, DN_HEADS), f32,
                                    minval=math.log(1e-3), maxval=math.log(1e-1)))
    return {
        "x": jax.random.normal(ks[0], (BATCH, SEQ, D), f32),
        "ffn_norm": gain(ks[1], (L, 2, D)),
        "ffn_w_gate": dense(ks[2], (L, 2, D, F), D),
        "ffn_w_up": dense(ks[3], (L, 2, D, F), D),
        "ffn_w_down": dense(ks[4], (L, 2, F, D), F),
        "mix_norm": gain(ks[5], (L, D)),
        "w_in": dense(ks[6], (L, D, P_IN), D),
        "b_gate": 0.01 * jax.random.normal(ks[7], (L, N_BRANCH * D), f32),
        "pool_w": dense(ks[8], (L, POOL_GROUPS, POOL_GROUP_DIM, POOL_GROUP_DIM), POOL_GROUP_DIM),
        "pool_scale": gain(ks[9], (L, BRANCH_WIDTH)),
        "dn_conv": dense(ks[10], (L, DN_CONV, 3 * BRANCH_WIDTH), DN_CONV),
        "dn_A_log": jnp.log(jax.random.uniform(ks[11], (L, DN_HEADS), f32, minval=1.0, maxval=16.0)),
        "dn_dt_bias": dt + jnp.log(-jnp.expm1(-dt)),
        "dn_out_norm": gain(ks[13], (L, DN_HEAD_DIM)),
        "w_branch": dense(ks[14], (L, N_BRANCH, BRANCH_WIDTH, D), BRANCH_WIDTH),
        "w_out": dense(ks[15], (L, D, D), D),
        "final_norm": gain(ks[16], (D,)),
    }


def reference(x, ffn_norm, ffn_w_gate, ffn_w_up, ffn_w_down, mix_norm, w_in, b_gate,
              pool_w, pool_scale, dn_conv, dn_A_log, dn_dt_bias, dn_out_norm,
              w_branch, w_out, final_norm):
    bsz, s, d_model = x.shape
    split_points = [int(p) for p in np.cumsum(IN_SPLITS)[:-1]]
    for l in range(DEPTH):
        hf = rms_norm(x, ffn_norm[l, 0])
        x = x + 0.5 * swiglu(hf, ffn_w_gate[l, 0], ffn_w_up[l, 0], ffn_w_down[l, 0])

        h = rms_norm(x, mix_norm[l])
        proj = h @ w_in[l]
        u_pool, dn_qkv, dn_z, dn_a, dn_b, sb_qkv, gate_logits = jnp.split(proj, split_points, axis=-1)
        y_pool = pool_mixer(u_pool, pool_w[l], pool_scale[l])
        y_dn = gated_deltanet(dn_qkv, dn_z, dn_a, dn_b, dn_conv[l], dn_A_log[l], dn_dt_bias[l], dn_out_norm[l])
        y_sb = stick_breaking_attention(sb_qkv)

        branches = jnp.stack([y_pool, y_dn, y_sb], axis=2)
        branch_d = jnp.einsum('bsnw,nwd->bsnd', branches, w_branch[l])
        gates = jax.nn.sigmoid((gate_logits + b_gate[l]).astype(jnp.float32)).astype(x.dtype)
        gates = gates.reshape(bsz, s, N_BRANCH, d_model)
        merged = jnp.sum(gates * branch_d, axis=2)
        x = x + merged @ w_out[l]

        hf = rms_norm(x, ffn_norm[l, 1])
        x = x + 0.5 * swiglu(hf, ffn_w_gate[l, 1], ffn_w_up[l, 1], ffn_w_down[l, 1])
    return rms_norm(x, final_norm)
```

```python
import functools

import jax
import jax.numpy as jnp
from jax import lax
from jax.experimental import pallas as pl
from jax.experimental.pallas import tpu as pltpu

F32 = jnp.float32
BF16 = jnp.bfloat16

EPS = 1e-6
POOL_WINDOWS = (2, 4, 8, 16)
HEAD_DIM = 128
DN_CONV = 4
SUBLANES = 8
LANES = 128
V7X_VMEM_BYTES = 64 * 1024 * 1024

NT_DIMS = (((1,), (1,)), ((), ()))
TN_DIMS = (((0,), (0,)), ((), ()))


def _cparams(semantics, vmem_bytes):
    assert vmem_bytes < V7X_VMEM_BYTES
    return pltpu.CompilerParams(dimension_semantics=semantics, vmem_limit_bytes=int(vmem_bytes))


def _resident(shape):
    zeros = (0,) * len(shape)
    return pl.BlockSpec(shape, lambda *_: zeros, pipeline_mode=pl.Buffered(1))


def _sigmoid(x):
    return 1.0 / (1.0 + jnp.exp(-x))


def _softplus(x):
    return jnp.maximum(x, 0.0) + jnp.log1p(jnp.exp(-jnp.abs(x)))


def _rms_norm(x, gain):
    return x * lax.rsqrt(jnp.mean(x * x, axis=-1, keepdims=True) + EPS) * gain


def _dot(a, b):
    return jnp.dot(a, b, preferred_element_type=F32)


def _split_bf16(x):
    hi = x.astype(BF16)
    lo = (x - hi.astype(F32)).astype(BF16)
    return hi, lo


def _ffn_kernel(x_ref, gain_ref, wg_ref, wu_ref, wd_ref, fgain_ref, o_ref, act_ref, *, f_chunk, final_norm):
    x = x_ref[...]
    h = _rms_norm(x, gain_ref[...]).astype(BF16)
    d_ff = wg_ref.shape[1]
    for c in range(d_ff // f_chunk):
        sl = slice(c * f_chunk, (c + 1) * f_chunk)
        g = _dot(h, wg_ref[:, sl])
        u = _dot(h, wu_ref[:, sl])
        act_ref[:, sl] = (g * _sigmoid(g) * u).astype(BF16)
    y = x + 0.5 * _dot(act_ref[...], wd_ref[...])
    if final_norm:
        y = _rms_norm(y, fgain_ref[...])
    o_ref[...] = y


def _ffn(x2d, gain, wg, wu, wd, final_gain, *, final_norm, tm=512, f_chunk=256):
    t, d = x2d.shape
    d_ff = wg.shape[1]
    assert t % tm == 0 and d_ff % f_chunk == 0
    vmem = (4 * tm * d * 4 + 3 * d * d_ff * 2 + tm * d_ff * 2 + 4 * tm * f_chunk * 4 + 2 * tm * d * 4) + (8 << 20)
    return pl.pallas_call(
        functools.partial(_ffn_kernel, f_chunk=f_chunk, final_norm=final_norm),
        grid=(t // tm,),
        in_specs=[
            pl.BlockSpec((tm, d), lambda i: (i, 0)),
            _resident((1, d)),
            _resident((d, d_ff)),
            _resident((d, d_ff)),
            _resident((d_ff, d)),
            _resident((1, d)),
        ],
        out_specs=pl.BlockSpec((tm, d), lambda i: (i, 0)),
        out_shape=jax.ShapeDtypeStruct((t, d), F32),
        scratch_shapes=[pltpu.VMEM((tm, d_ff), BF16)],
        compiler_params=_cparams(("parallel",), vmem),
        name="ffn",
    )(x2d, gain, wg, wu, wd, final_gain)


def _proj_kernel(x_ref, gain_ref, w_ref, wab_ref, o_ref, ab_ref, *, n_chunk, q_cols, q_scale):
    h = _rms_norm(x_ref[...], gain_ref[...]).astype(BF16)
    n_main = w_ref.shape[1]
    for c in range(n_main // n_chunk):
        lo, hi = c * n_chunk, (c + 1) * n_chunk
        p = _dot(h, w_ref[:, lo:hi])
        if q_cols[0] <= lo and hi <= q_cols[1]:
            p = p * q_scale
        else:
            assert hi <= q_cols[0] or lo >= q_cols[1]
        o_ref[:, lo:hi] = p.astype(BF16)
    ab_ref[...] = _dot(h, wab_ref[...])


def _proj(x2d, gain, w_main, w_ab, *, q_cols, q_scale, tm=512, n_chunk=512):
    t, d = x2d.shape
    n_main = w_main.shape[1]
    assert t % tm == 0 and n_main % n_chunk == 0
    vmem = 2 * tm * d * 4 + d * n_main * 2 + d * LANES * 2 + 2 * tm * n_main * 2 + 2 * tm * LANES * 4 + (8 << 20)
    return pl.pallas_call(
        functools.partial(_proj_kernel, n_chunk=n_chunk, q_cols=q_cols, q_scale=q_scale),
        grid=(t // tm,),
        in_specs=[
            pl.BlockSpec((tm, d), lambda i: (i, 0)),
            _resident((1, d)),
            _resident((d, n_main)),
            _resident((d, LANES)),
        ],
        out_specs=[
            pl.BlockSpec((tm, n_main), lambda i: (i, 0)),
            pl.BlockSpec((tm, LANES), lambda i: (i, 0)),
        ],
        out_shape=[
            jax.ShapeDtypeStruct((t, n_main), BF16),
            jax.ShapeDtypeStruct((t, LANES), F32),
        ],
        compiler_params=_cparams(("parallel",), vmem),
        name="in_proj",
    )(x2d, gain, w_main, w_ab)


def _shift_rows(x, s, row):
    return jnp.where(row < s, 0.0, pltpu.roll(x, s, 0))


def _pool_kernel(u_ref, w_ref, scale_ref, o_ref):
    seq = u_ref.shape[0]
    gdim = w_ref.shape[1]
    row = lax.broadcasted_iota(jnp.int32, (seq, gdim), 0)
    pos = (row + 1).astype(F32)
    for gi, win in enumerate(POOL_WINDOWS):
        sl = slice(gi * gdim, (gi + 1) * gdim)
        u = u_ref[:, sl].astype(F32)
        total = u
        span = 1
        while span < win:
            total = total + _shift_rows(total, span, row)
            span *= 2
        pooled = total / jnp.minimum(pos, float(win)) - u
        mixed = _dot(pooled.astype(BF16), w_ref[gi])
        o_ref[:, sl] = (mixed * scale_ref[:, sl]).astype(BF16)


def _pool(proj3, pool_w, scale, *, col_block):
    b, s, _ = proj3.shape
    g, gdim, _ = pool_w.shape
    width = g * gdim
    vmem = 4 * s * width * 2 + 12 * s * gdim * 4 + (8 << 20)
    return pl.pallas_call(
        _pool_kernel,
        grid=(b,),
        in_specs=[
            pl.BlockSpec((None, s, width), lambda i: (i, 0, col_block)),
            _resident((g, gdim, gdim)),
            _resident((1, width)),
        ],
        out_specs=pl.BlockSpec((None, s, width), lambda i: (i, 0, 0)),
        out_shape=jax.ShapeDtypeStruct((b, s, width), BF16),
        compiler_params=_cparams(("parallel",), vmem),
        name="pool",
    )(proj3, pool_w, scale)


def _dn_kernel(qkv_ref, z_ref, ab_ref, abt_ref, conv_ref, prow_ref, pcol_ref, gain_ref, o_ref,
               state_ref, halo_ref, *, heads):
    c = qkv_ref.shape[0]
    hd = HEAD_DIM
    width = heads * hd

    @pl.when(pl.program_id(1) == 0)
    def _():
        state_ref[...] = jnp.zeros_like(state_ref)
        halo_ref[...] = jnp.zeros_like(halo_ref)

    x = qkv_ref[...].astype(F32)
    prev = halo_ref[...]
    row8 = lax.broadcasted_iota(jnp.int32, prev.shape, 0)
    acc = x * conv_ref[DN_CONV - 1:DN_CONV, :]
    for s in range(1, DN_CONV):
        rolled = pltpu.roll(x, s, 0)
        top = jnp.where(row8 < s, pltpu.roll(prev, s, 0), rolled[:SUBLANES])
        shifted = jnp.concatenate([top, rolled[SUBLANES:]], axis=0)
        acc = acc + shifted * conv_ref[DN_CONV - 1 - s:DN_CONV - s, :]
    halo_ref[...] = x[c - SUBLANES:]
    qkv = acc * _sigmoid(acc)

    ab = ab_ref[...]
    abt = abt_ref[...]
    g_col = -jnp.exp(prow_ref[0:1, :]) * _softplus(ab + prow_ref[1:2, :])
    g_row = -jnp.exp(pcol_ref[:, 0:1]) * _softplus(abt + pcol_ref[:, 1:2])
    beta_col = _sigmoid(ab)
    ri = lax.broadcasted_iota(jnp.int32, (c, c), 0)
    ci = lax.broadcasted_iota(jnp.int32, (c, c), 1)
    incl = ri >= ci
    lower_ones = jnp.where(incl, 1.0, 0.0).astype(BF16)
    upper_ones = jnp.where(ri <= ci, 1.0, 0.0).astype(BF16)
    gh, gl = _split_bf16(g_col)
    gc_col = _dot(lower_ones, gh) + _dot(lower_ones, gl)
    gh, gl = _split_bf16(g_row)
    gc_row = _dot(gh, upper_ones) + _dot(gl, upper_ones)
    gc_last = gc_col[c - 1:c, :]
    egc_col = jnp.exp(gc_col)
    erest_col = jnp.exp(gc_last - gc_col)
    elast = jnp.exp(gc_last)
    eye = jnp.where(ri == ci, 1.0, 0.0)

    for h in range(heads):
        q = qkv[:, h * hd:(h + 1) * hd]
        k = qkv[:, width + h * hd:width + (h + 1) * hd]
        v = qkv[:, 2 * width + h * hd:2 * width + (h + 1) * hd]
        q = q * lax.rsqrt(jnp.sum(q * q, axis=-1, keepdims=True) + EPS) * (hd ** -0.5)
        k = k * lax.rsqrt(jnp.sum(k * k, axis=-1, keepdims=True) + EPS)
        beta = beta_col[:, heads + h:heads + h + 1]
        kb = k * beta
        diff = gc_col[:, h:h + 1] - gc_row[h:h + 1, :]
        decay = jnp.where(incl, jnp.exp(jnp.where(incl, diff, 0.0)), 0.0)
        k16 = k.astype(BF16)
        lmat = lax.dot_general(kb.astype(BF16), k16, NT_DIMS, preferred_element_type=F32)
        lmat = lmat * jnp.where(ri > ci, decay, 0.0)
        attn = lax.dot_general(q.astype(BF16), k16, NT_DIMS, preferred_element_type=F32) * decay

        inv = eye - lmat
        power = lmat
        span = 1
        while span < c // 2:
            p16 = power.astype(BF16)
            power = _dot(p16, p16)
            inv = inv + _dot(inv.astype(BF16), power.astype(BF16))
            span *= 2
        rhs = jnp.concatenate([v * beta, kb * egc_col[:, h:h + 1]], axis=1)
        sol = _dot(inv.astype(BF16), rhs.astype(BF16))
        u, w = sol[:, :hd], sol[:, hd:]

        state = state_ref[h]
        s16 = state.astype(BF16)
        v_new = u - _dot(w.astype(BF16), s16)
        vn16 = v_new.astype(BF16)
        o = _dot((q * egc_col[:, h:h + 1]).astype(BF16), s16) + _dot(attn.astype(BF16), vn16)
        k_dec = (k * erest_col[:, h:h + 1]).astype(BF16)
        state_ref[h] = state * elast[:, h:h + 1] + lax.dot_general(
            k_dec, vn16, TN_DIMS, preferred_element_type=F32)

        o = o * lax.rsqrt(jnp.mean(o * o, axis=-1, keepdims=True) + EPS) * gain_ref[...]
        zh = z_ref[:, h * hd:(h + 1) * hd].astype(F32)
        o_ref[:, h * hd:(h + 1) * hd] = (o * (zh * _sigmoid(zh))).astype(BF16)


def _deltanet(proj3, ab3, abt3, conv_w, prow, pcol, gain, *, heads, qkv_block, z_block, chunk=128):
    b, s, _ = proj3.shape
    width = heads * HEAD_DIM
    assert s % chunk == 0
    vmem = 24 * chunk * 3 * width * 4 + (16 << 20)
    return pl.pallas_call(
        functools.partial(_dn_kernel, heads=heads),
        grid=(b, s // chunk),
        in_specs=[
            pl.BlockSpec((None, chunk, 3 * width), lambda i, n: (i, n, qkv_block)),
            pl.BlockSpec((None, chunk, width), lambda i, n: (i, n, z_block)),
            pl.BlockSpec((None, chunk, LANES), lambda i, n: (i, n, 0)),
            pl.BlockSpec((None, SUBLANES, chunk), lambda i, n: (i, 0, n)),
            _resident((DN_CONV, 3 * width)),
            _resident((2, LANES)),
            _resident((SUBLANES, 2)),
            _resident((1, HEAD_DIM)),
        ],
        out_specs=pl.BlockSpec((None, chunk, width), lambda i, n: (i, n, 0)),
        out_shape=jax.ShapeDtypeStruct((b, s, width), BF16),
        scratch_shapes=[
            pltpu.VMEM((heads, HEAD_DIM, HEAD_DIM), F32),
            pltpu.VMEM((SUBLANES, 3 * width), F32),
        ],
        compiler_params=_cparams(("parallel", "arbitrary"), vmem),
        name="deltanet",
    )(proj3, proj3, ab3, abt3, conv_w, prow, pcol, gain)


def _sb_block(q, k, v, carry, upper, causal):
    run, acc = carry
    z = lax.dot_general(q, k, NT_DIMS, preferred_element_type=F32)
    sp = _softplus(z)
    log_not = -sp if causal is None else jnp.where(causal, -sp, 0.0)
    hi, lo = _split_bf16(log_not)
    tail = _dot(hi, upper) + _dot(lo, upper) + run
    wgt = jnp.exp(z - sp + tail)
    if causal is not None:
        wgt = jnp.where(causal, wgt, 0.0)
    acc = acc + _dot(wgt.astype(BF16), v)
    run = run + jnp.sum(log_not, axis=-1, keepdims=True)
    return run, acc


def _sb_kernel(q_ref, k_ref, v_ref, o_ref, *, tk):
    tq = q_ref.shape[0]
    assert tq == tk
    qi = pl.program_id(2)
    q = q_ref[...]
    ri = lax.broadcasted_iota(jnp.int32, (tk, tk), 0)
    ci = lax.broadcasted_iota(jnp.int32, (tk, tk), 1)
    upper = jnp.where(ri > ci, 1.0, 0.0).astype(BF16)
    diag0 = pl.multiple_of(qi * tk, tk)
    carry = (jnp.zeros((tq, 1), F32), jnp.zeros((tq, HEAD_DIM), F32))
    carry = _sb_block(q, k_ref[pl.ds(diag0, tk), :], v_ref[pl.ds(diag0, tk), :], carry, upper, ci < ri)

    def body(i, carry):
        k0 = pl.multiple_of((qi - 1 - i) * tk, tk)
        return _sb_block(q, k_ref[pl.ds(k0, tk), :], v_ref[pl.ds(k0, tk), :], carry, upper, None)

    _, acc = lax.fori_loop(0, qi, body, carry)
    o_ref[...] = acc.astype(BF16)


def _stickbreak(proj3, *, heads, q_block, tq=256):
    b, s, _ = proj3.shape
    hd = HEAD_DIM
    assert s % tq == 0
    vmem = 4 * s * hd * 2 + 24 * tq * tq * 4 + (8 << 20)
    return pl.pallas_call(
        functools.partial(_sb_kernel, tk=tq),
        grid=(b, heads, s // tq),
        in_specs=[
            pl.BlockSpec((None, tq, hd), lambda i, h, t: (i, t, q_block + h)),
            pl.BlockSpec((None, s, hd), lambda i, h, t: (i, 0, q_block + heads + h)),
            pl.BlockSpec((None, s, hd), lambda i, h, t: (i, 0, q_block + 2 * heads + h)),
        ],
        out_specs=pl.BlockSpec((None, tq, hd), lambda i, h, t: (i, t, h)),
        out_shape=jax.ShapeDtypeStruct((b, s, heads * hd), BF16),
        compiler_params=_cparams(("parallel", "parallel", "arbitrary"), vmem),
        name="stickbreak",
    )(proj3, proj3, proj3)


def _merge_kernel(x_ref, yp_ref, yd_ref, ys_ref, gl_ref, bg_ref, wb_ref, wo_ref, o_ref):
    d = x_ref.shape[1]
    merged = None
    for i, y_ref in enumerate((yp_ref, yd_ref, ys_ref)):
        sl = slice(i * d, (i + 1) * d)
        gate = _sigmoid(gl_ref[:, sl].astype(F32) + bg_ref[:, sl])
        term = gate * _dot(y_ref[...], wb_ref[i])
        merged = term if merged is None else merged + term
    o_ref[...] = x_ref[...] + _dot(merged.astype(BF16), wo_ref[...])


def _merge(x2d, y_pool, y_dn, y_sb, proj2, b_gate, w_branch, w_out, *, tm=512):
    t, d = x2d.shape
    nb, bw, _ = w_branch.shape
    assert t % tm == 0
    vmem = (4 * tm * d * 4 + 6 * tm * bw * 2 + 2 * tm * nb * d * 2 + nb * bw * d * 2 + d * d * 2
            + 6 * tm * d * 4 + (8 << 20))
    y_spec = pl.BlockSpec((tm, bw), lambda i: (i, 0))
    return pl.pallas_call(
        _merge_kernel,
        grid=(t // tm,),
        in_specs=[
            pl.BlockSpec((tm, d), lambda i: (i, 0)),
            y_spec, y_spec, y_spec,
            pl.BlockSpec((tm, nb * d), lambda i: (i, 0)),
            _resident((1, nb * d)),
            _resident((nb, bw, d)),
            _resident((d, d)),
        ],
        out_specs=pl.BlockSpec((tm, d), lambda i: (i, 0)),
        out_shape=jax.ShapeDtypeStruct((t, d), F32),
        compiler_params=_cparams(("parallel",), vmem),
        name="merge",
    )(x2d, y_pool, y_dn, y_sb, proj2, b_gate, w_branch, w_out)


def kernel(x, ffn_norm, ffn_w_gate, ffn_w_up, ffn_w_down, mix_norm, w_in, b_gate, pool_w, pool_scale, dn_conv,
           dn_A_log, dn_dt_bias, dn_out_norm, w_branch, w_out, final_norm):
    bsz, seq, d = x.shape
    depth = w_in.shape[0]
    n_branch, bw = w_branch.shape[1], w_branch.shape[2]
    heads = bw // HEAD_DIM
    t = bsz * seq

    splits = (bw, 3 * bw, bw, heads, heads, 3 * bw, n_branch * d)
    offs = [0]
    for w in splits:
        offs.append(offs[-1] + w)
    o_pool, o_dnqkv, o_z, o_a, o_b, o_sb, o_gate, o_end = offs
    assert o_end == w_in.shape[2]
    c_gate, c_dnqkv = 0, n_branch * d
    c_sb = c_dnqkv + 3 * bw
    c_pool = c_sb + 3 * bw
    c_z = c_pool + bw
    n_main = c_z + bw

    x2 = x.reshape(t, d)
    for l in range(depth):
        wl = w_in[l]
        w_main = jnp.concatenate(
            [wl[:, o_gate:o_end], wl[:, o_dnqkv:o_z], wl[:, o_sb:o_gate], wl[:, o_pool:o_dnqkv], wl[:, o_z:o_a]],
            axis=1).astype(BF16)
        w_ab = jnp.pad(wl[:, o_a:o_sb], ((0, 0), (0, LANES - 2 * heads))).astype(BF16)
        row = lambda v: v.reshape(1, -1)

        x2 = _ffn(x2, row(ffn_norm[l, 0]), ffn_w_gate[l, 0].astype(BF16), ffn_w_up[l, 0].astype(BF16),
                  ffn_w_down[l, 0].astype(BF16), row(final_norm), final_norm=False)

        proj, ab = _proj(x2, row(mix_norm[l]), w_main, w_ab,
                         q_cols=(c_sb, c_sb + bw), q_scale=HEAD_DIM ** -0.5)
        proj3 = proj.reshape(bsz, seq, n_main)
        ab3 = ab.reshape(bsz, seq, LANES)
        abt3 = jnp.swapaxes(ab3[:, :, :SUBLANES], 1, 2)

        y_pool = _pool(proj3, pool_w[l].astype(BF16), row(pool_scale[l]), col_block=c_pool // bw)

        pad_lanes = LANES - heads
        prow = jnp.stack([jnp.pad(dn_A_log[l], (0, pad_lanes)), jnp.pad(dn_dt_bias[l], (0, pad_lanes))])
        pcol = jnp.pad(jnp.stack([dn_A_log[l], dn_dt_bias[l]], axis=1), ((0, SUBLANES - heads), (0, 0)))
        y_dn = _deltanet(proj3, ab3, abt3, dn_conv[l], prow, pcol, row(dn_out_norm[l]), heads=heads,
                         qkv_block=c_dnqkv // (3 * bw), z_block=c_z // bw)

        y_sb = _stickbreak(proj3, heads=heads, q_block=c_sb // HEAD_DIM)

        x2 = _merge(x2, y_pool.reshape(t, bw), y_dn.reshape(t, bw), y_sb.reshape(t, bw), proj,
                    row(b_gate[l]), w_branch[l].astype(BF16), w_out[l].astype(BF16))

        x2 = _ffn(x2, row(ffn_norm[l, 1]), ffn_w_gate[l, 1].astype(BF16), ffn_w_up[l, 1].astype(BF16),
                  ffn_w_down[l, 1].astype(BF16), row(final_norm), final_norm=(l == depth - 1))
    return x2.reshape(bsz, seq, d)
```

```python
import functools

import jax
import jax.numpy as jnp
from jax import lax
from jax.experimental import pallas as pl
from jax.experimental.pallas import tpu as pltpu

F32 = jnp.float32
BF16 = jnp.bfloat16

EPS = 1e-6
POOL_WINDOWS = (2, 4, 8, 16)
HEAD_DIM = 128
DN_CONV = 4
SUBLANES = 8
LANES = 128
V7X_VMEM_BYTES = 64 * 1024 * 1024

NT_DIMS = (((1,), (1,)), ((), ()))
TN_DIMS = (((0,), (0,)), ((), ()))


def _cparams(semantics, vmem_bytes):
    assert vmem_bytes < V7X_VMEM_BYTES
    return pltpu.CompilerParams(dimension_semantics=semantics, vmem_limit_bytes=int(vmem_bytes))


def _resident(shape):
    zeros = (0,) * len(shape)
    return pl.BlockSpec(shape, lambda *_: zeros, pipeline_mode=pl.Buffered(1))


def _sigmoid(x):
    return 1.0 / (1.0 + jnp.exp(-x))


def _softplus(x):
    return jnp.maximum(x, 0.0) + jnp.log(1.0 + jnp.exp(-jnp.abs(x)))


def _rms_norm(x, gain):
    return x * lax.rsqrt(jnp.mean(x * x, axis=-1, keepdims=True) + EPS) * gain


def _dot(a, b):
    return jnp.dot(a, b, preferred_element_type=F32)


def _split_bf16(x):
    hi = x.astype(BF16)
    lo = (x - hi.astype(F32)).astype(BF16)
    return hi, lo


def _ffn_kernel(x_ref, gain_ref, wg_ref, wu_ref, wd_ref, fgain_ref, o_ref, act_ref, *, f_chunk, final_norm):
    x = x_ref[...]
    h = _rms_norm(x, gain_ref[...]).astype(BF16)
    d_ff = wg_ref.shape[1]
    for c in range(d_ff // f_chunk):
        sl = slice(c * f_chunk, (c + 1) * f_chunk)
        g = _dot(h, wg_ref[:, sl])
        u = _dot(h, wu_ref[:, sl])
        act_ref[:, sl] = (g * _sigmoid(g) * u).astype(BF16)
    y = x + 0.5 * _dot(act_ref[...], wd_ref[...])
    if final_norm:
        y = _rms_norm(y, fgain_ref[...])
    o_ref[...] = y


def _ffn(x2d, gain, wg, wu, wd, final_gain, *, final_norm, tm=512, f_chunk=256):
    t, d = x2d.shape
    d_ff = wg.shape[1]
    assert t % tm == 0 and d_ff % f_chunk == 0
    vmem = (4 * tm * d * 4 + 3 * d * d_ff * 2 + tm * d_ff * 2 + 4 * tm * f_chunk * 4 + 2 * tm * d * 4) + (8 << 20)
    return pl.pallas_call(
        functools.partial(_ffn_kernel, f_chunk=f_chunk, final_norm=final_norm),
        grid=(t // tm,),
        in_specs=[
            pl.BlockSpec((tm, d), lambda i: (i, 0)),
            _resident((1, d)),
            _resident((d, d_ff)),
            _resident((d, d_ff)),
            _resident((d_ff, d)),
            _resident((1, d)),
        ],
        out_specs=pl.BlockSpec((tm, d), lambda i: (i, 0)),
        out_shape=jax.ShapeDtypeStruct((t, d), F32),
        scratch_shapes=[pltpu.VMEM((tm, d_ff), BF16)],
        compiler_params=_cparams(("parallel",), vmem),
        name="ffn",
    )(x2d, gain, wg, wu, wd, final_gain)


def _proj_kernel(x_ref, gain_ref, w_ref, wab_ref, o_ref, ab_ref, *, n_chunk, q_cols, q_scale):
    h = _rms_norm(x_ref[...], gain_ref[...]).astype(BF16)
    n_main = w_ref.shape[1]
    for c in range(n_main // n_chunk):
        lo, hi = c * n_chunk, (c + 1) * n_chunk
        p = _dot(h, w_ref[:, lo:hi])
        if q_cols[0] <= lo and hi <= q_cols[1]:
            p = p * q_scale
        else:
            assert hi <= q_cols[0] or lo >= q_cols[1]
        o_ref[:, lo:hi] = p.astype(BF16)
    ab_ref[...] = _dot(h, wab_ref[...])


def _proj(x2d, gain, w_main, w_ab, *, q_cols, q_scale, tm=512, n_chunk=512):
    t, d = x2d.shape
    n_main = w_main.shape[1]
    assert t % tm == 0 and n_main % n_chunk == 0
    vmem = 2 * tm * d * 4 + d * n_main * 2 + d * LANES * 2 + 2 * tm * n_main * 2 + 2 * tm * LANES * 4 + (8 << 20)
    return pl.pallas_call(
        functools.partial(_proj_kernel, n_chunk=n_chunk, q_cols=q_cols, q_scale=q_scale),
        grid=(t // tm,),
        in_specs=[
            pl.BlockSpec((tm, d), lambda i: (i, 0)),
            _resident((1, d)),
            _resident((d, n_main)),
            _resident((d, LANES)),
        ],
        out_specs=[
            pl.BlockSpec((tm, n_main), lambda i: (i, 0)),
            pl.BlockSpec((tm, LANES), lambda i: (i, 0)),
        ],
        out_shape=[
            jax.ShapeDtypeStruct((t, n_main), BF16),
            jax.ShapeDtypeStruct((t, LANES), F32),
        ],
        compiler_params=_cparams(("parallel",), vmem),
        name="in_proj",
    )(x2d, gain, w_main, w_ab)


def _shift_rows(x, s, row):
    return jnp.where(row < s, 0.0, pltpu.roll(x, s, 0))


def _pool_kernel(u_ref, w_ref, scale_ref, o_ref):
    seq = u_ref.shape[0]
    gdim = w_ref.shape[1]
    row = lax.broadcasted_iota(jnp.int32, (seq, gdim), 0)
    pos = (row + 1).astype(F32)
    for gi, win in enumerate(POOL_WINDOWS):
        sl = slice(gi * gdim, (gi + 1) * gdim)
        u = u_ref[:, sl].astype(F32)
        total = u
        span = 1
        while span < win:
            total = total + _shift_rows(total, span, row)
            span *= 2
        pooled = total / jnp.minimum(pos, float(win)) - u
        mixed = _dot(pooled.astype(BF16), w_ref[gi])
        o_ref[:, sl] = (mixed * scale_ref[:, sl]).astype(BF16)


def _pool(proj3, pool_w, scale, *, col_block):
    b, s, _ = proj3.shape
    g, gdim, _ = pool_w.shape
    width = g * gdim
    vmem = 4 * s * width * 2 + 12 * s * gdim * 4 + (8 << 20)
    return pl.pallas_call(
        _pool_kernel,
        grid=(b,),
        in_specs=[
            pl.BlockSpec((None, s, width), lambda i: (i, 0, col_block)),
            _resident((g, gdim, gdim)),
            _resident((1, width)),
        ],
        out_specs=pl.BlockSpec((None, s, width), lambda i: (i, 0, 0)),
        out_shape=jax.ShapeDtypeStruct((b, s, width), BF16),
        compiler_params=_cparams(("parallel",), vmem),
        name="pool",
    )(proj3, pool_w, scale)


def _row_sums(x):
    ones = jnp.ones((x.shape[1], LANES), BF16)
    hi, lo = _split_bf16(x)
    return _dot(hi, ones) + _dot(lo, ones)


def _dn_kernel(qkv_ref, z_ref, ab_ref, abt_ref, conv_ref, prow_ref, pcol_ref, gain_ref, o_ref,
               state_ref, halo_ref, *, heads):
    nb, c = qkv_ref.shape[0], qkv_ref.shape[1]
    hd = HEAD_DIM
    width = heads * hd
    halo = SUBLANES

    @pl.when(pl.program_id(1) == 0)
    def _():
        state_ref[...] = jnp.zeros_like(state_ref)
        halo_ref[...] = jnp.zeros_like(halo_ref)

    ri = lax.broadcasted_iota(jnp.int32, (c, c), 0)
    ci = lax.broadcasted_iota(jnp.int32, (c, c), 1)
    incl = ri >= ci
    strict = ri > ci
    lower_ones = jnp.where(incl, 1.0, 0.0).astype(BF16)
    upper_ones = jnp.where(ri <= ci, 1.0, 0.0).astype(BF16)
    eye = jnp.where(ri == ci, 1.0, 0.0)

    si = lax.broadcasted_iota(jnp.int32, ((DN_CONV - 1) * c, c), 0)
    sj = lax.broadcasted_iota(jnp.int32, ((DN_CONV - 1) * c, c), 1)
    shift_ones = jnp.zeros(si.shape, F32)
    for s in range(1, DN_CONV):
        shift_ones = jnp.where((si - (s - 1) * c == sj + s) & (si < s * c), 1.0, shift_ones)
    shift_ones = shift_ones.astype(BF16)
    row8 = lax.broadcasted_iota(jnp.int32, (halo, 3 * width), 0)

    qs, ks, vs, betas, gcols, grows, egcs, erests, elasts = [], [], [], [], [], [], [], [], []
    for bi in range(nb):
        x16 = qkv_ref[bi]
        x = x16.astype(F32)
        shifted = _dot(shift_ones, x16)
        prev = halo_ref[bi]
        acc = x * conv_ref[DN_CONV - 1:DN_CONV, :]
        top = jnp.zeros((halo, 3 * width), F32)
        for s in range(1, DN_CONV):
            tap = conv_ref[DN_CONV - 1 - s:DN_CONV - s, :]
            acc = acc + shifted[(s - 1) * c:s * c] * tap
            top = top + jnp.where(row8 < s, pltpu.roll(prev, s, 0), 0.0) * tap
        acc = jnp.concatenate([acc[:halo] + top, acc[halo:]], axis=0)
        halo_ref[bi] = x[c - halo:]
        qkv = acc * _sigmoid(acc)

        ab = ab_ref[bi]
        abt = abt_ref[bi]
        g_col = -jnp.exp(prow_ref[0:1, :]) * _softplus(ab + prow_ref[1:2, :])
        g_row = -jnp.exp(pcol_ref[:, 0:1]) * _softplus(abt + pcol_ref[:, 1:2])
        beta_col = _sigmoid(ab)
        gh, gl = _split_bf16(g_col)
        gc_col = _dot(lower_ones, gh) + _dot(lower_ones, gl)
        gh, gl = _split_bf16(g_row)
        gc_row = _dot(gh, upper_ones) + _dot(gl, upper_ones)
        gc_last = gc_col[c - 1:c, :]
        egc_col = jnp.exp(gc_col)
        erest_col = jnp.exp(gc_last - gc_col)
        elast = jnp.exp(gc_last)
        for h in range(heads):
            qs.append(qkv[:, h * hd:(h + 1) * hd])
            ks.append(qkv[:, width + h * hd:width + (h + 1) * hd])
            vs.append(qkv[:, 2 * width + h * hd:2 * width + (h + 1) * hd])
            betas.append(beta_col[:, heads + h:heads + h + 1])
            gcols.append(gc_col[:, h:h + 1])
            grows.append(gc_row[h:h + 1, :])
            egcs.append(egc_col[:, h:h + 1])
            erests.append(erest_col[:, h:h + 1])
            elasts.append(elast[:, h:h + 1])
    groups = range(nb * heads)

    qs = [q * lax.rsqrt(_row_sums(q * q) + EPS) * (hd ** -0.5) for q in qs]
    ks = [k * lax.rsqrt(_row_sums(k * k) + EPS) for k in ks]
    kbs = [k * b for k, b in zip(ks, betas)]
    k16 = [k.astype(BF16) for k in ks]
    decays = [jnp.where(incl, jnp.exp(jnp.where(incl, gcol - grow, 0.0)), 0.0) for gcol, grow in zip(gcols, grows)]
    lmats = [lax.dot_general(kb.astype(BF16), k, NT_DIMS, preferred_element_type=F32) * jnp.where(strict, dec, 0.0)
             for kb, k, dec in zip(kbs, k16, decays)]
    attns = [(lax.dot_general(q.astype(BF16), k, NT_DIMS, preferred_element_type=F32) * dec).astype(BF16)
             for q, k, dec in zip(qs, k16, decays)]

    invs = [eye - l for l in lmats]
    pw16 = [l.astype(BF16) for l in lmats]
    span = 1
    while span < c // 2:
        pw16 = [_dot(p, p).astype(BF16) for p in pw16]
        invs = [x + _dot(x.astype(BF16), p) for x, p in zip(invs, pw16)]
        span *= 2
    rhs = [jnp.concatenate([v * b, kb * e], axis=1).astype(BF16) for v, b, kb, e in zip(vs, betas, kbs, egcs)]
    sols = [_dot(x.astype(BF16), r) for x, r in zip(invs, rhs)]

    states = [state_ref[g] for g in groups]
    s16 = [s.astype(BF16) for s in states]
    vn16 = [(sol[:, :hd] - _dot(sol[:, hd:].astype(BF16), s)).astype(BF16) for sol, s in zip(sols, s16)]
    outs = [_dot((q * e).astype(BF16), s) + _dot(a, vn) for q, e, s, a, vn in zip(qs, egcs, s16, attns, vn16)]
    for g in groups:
        k_dec = (ks[g] * erests[g]).astype(BF16)
        state_ref[g] = states[g] * elasts[g] + lax.dot_general(k_dec, vn16[g], TN_DIMS, preferred_element_type=F32)

    outs = [o * lax.rsqrt(_row_sums(o * o) * (1.0 / hd) + EPS) * gain_ref[...] for o in outs]
    for g in groups:
        bi, h = divmod(g, heads)
        zh = z_ref[bi, :, h * hd:(h + 1) * hd].astype(F32)
        o_ref[bi, :, h * hd:(h + 1) * hd] = (outs[g] * (zh * _sigmoid(zh))).astype(BF16)


def _deltanet(proj3, ab3, abt3, conv_w, prow, pcol, gain, *, heads, qkv_block, z_block, chunk=128, nb=2):
    b, s, _ = proj3.shape
    width = heads * HEAD_DIM
    assert s % chunk == 0 and b % nb == 0
    vmem = nb * 24 * chunk * 3 * width * 4 + (16 << 20)
    return pl.pallas_call(
        functools.partial(_dn_kernel, heads=heads),
        grid=(b // nb, s // chunk),
        in_specs=[
            pl.BlockSpec((nb, chunk, 3 * width), lambda i, n: (i, n, qkv_block)),
            pl.BlockSpec((nb, chunk, width), lambda i, n: (i, n, z_block)),
            pl.BlockSpec((nb, chunk, LANES), lambda i, n: (i, n, 0)),
            pl.BlockSpec((nb, SUBLANES, chunk), lambda i, n: (i, 0, n)),
            _resident((DN_CONV, 3 * width)),
            _resident((2, LANES)),
            _resident((SUBLANES, 2)),
            _resident((1, HEAD_DIM)),
        ],
        out_specs=pl.BlockSpec((nb, chunk, width), lambda i, n: (i, n, 0)),
        out_shape=jax.ShapeDtypeStruct((b, s, width), BF16),
        scratch_shapes=[
            pltpu.VMEM((nb * heads, HEAD_DIM, HEAD_DIM), F32),
            pltpu.VMEM((nb, SUBLANES, 3 * width), F32),
        ],
        compiler_params=_cparams(("parallel", "arbitrary"), vmem),
        name="deltanet",
    )(proj3, proj3, ab3, abt3, conv_w, prow, pcol, gain)


def _sb_blocks(qs, ks, vs, runs, accs, upper2, causal):
    zs = [lax.dot_general(q, k, NT_DIMS, preferred_element_type=F32) for q, k in zip(qs, ks)]
    sps = [_softplus(z) for z in zs]
    masked = sps if causal is None else [jnp.where(causal, sp, 0.0) for sp in sps]
    splits = [jnp.concatenate(_split_bf16(m), axis=1) for m in masked]
    tails = [_dot(hl, upper2) + run for hl, run in zip(splits, runs)]
    wgts = [jnp.exp(z - sp - tail) for z, sp, tail in zip(zs, sps, tails)]
    if causal is not None:
        wgts = [jnp.where(causal, w, 0.0) for w in wgts]
    accs = [acc + _dot(w.astype(BF16), v) for acc, w, v in zip(accs, wgts, vs)]
    runs = [run + jnp.sum(m, axis=-1, keepdims=True) for run, m in zip(runs, masked)]
    return runs, accs


def _sb_kernel(q_ref, k_ref, v_ref, o_ref, *, heads, tk):
    tq = q_ref.shape[0]
    hd = HEAD_DIM
    assert tq == tk
    qi = pl.program_id(1)
    ri = lax.broadcasted_iota(jnp.int32, (tk, tk), 0)
    ci = lax.broadcasted_iota(jnp.int32, (tk, tk), 1)
    upper = jnp.where(ri > ci, 1.0, 0.0).astype(BF16)
    upper = jnp.concatenate([upper, upper], axis=0)
    qs = [q_ref[:, h * hd:(h + 1) * hd] for h in range(heads)]

    def block(k0, runs, accs, causal):
        ks = [k_ref[pl.ds(k0, tk), h * hd:(h + 1) * hd] for h in range(heads)]
        vs = [v_ref[pl.ds(k0, tk), h * hd:(h + 1) * hd] for h in range(heads)]
        return _sb_blocks(qs, ks, vs, runs, accs, upper, causal)

    runs = [jnp.zeros((tq, 1), F32)] * heads
    accs = [jnp.zeros((tq, hd), F32)] * heads
    carry = block(pl.multiple_of(qi * tk, tk), runs, accs, ci < ri)

    def body(i, carry):
        runs, accs = carry
        runs, accs = block(pl.multiple_of((qi - 1 - i) * tk, tk), list(runs), list(accs), None)
        return tuple(runs), tuple(accs)

    _, accs = lax.fori_loop(0, qi, body, (tuple(carry[0]), tuple(carry[1])))
    for h in range(heads):
        o_ref[:, h * hd:(h + 1) * hd] = accs[h].astype(BF16)


def _stickbreak(proj3, *, heads, q_block, tq=256):
    b, s, _ = proj3.shape
    width = heads * HEAD_DIM
    assert s % tq == 0 and q_block % heads == 0
    vmem = 4 * s * width * 2 + heads * 24 * tq * tq * 4 + (8 << 20)
    return pl.pallas_call(
        functools.partial(_sb_kernel, heads=heads, tk=tq),
        grid=(b, s // tq),
        in_specs=[
            pl.BlockSpec((None, tq, width), lambda i, t: (i, t, q_block // heads)),
            pl.BlockSpec((None, s, width), lambda i, t: (i, 0, q_block // heads + 1)),
            pl.BlockSpec((None, s, width), lambda i, t: (i, 0, q_block // heads + 2)),
        ],
        out_specs=pl.BlockSpec((None, tq, width), lambda i, t: (i, t, 0)),
        out_shape=jax.ShapeDtypeStruct((b, s, width), BF16),
        compiler_params=_cparams(("parallel", "arbitrary"), vmem),
        name="stickbreak",
    )(proj3, proj3, proj3)


def _merge_kernel(x_ref, yp_ref, yd_ref, ys_ref, gl_ref, bg_ref, wb_ref, wo_ref, o_ref):
    d = x_ref.shape[1]
    merged = None
    for i, y_ref in enumerate((yp_ref, yd_ref, ys_ref)):
        sl = slice(i * d, (i + 1) * d)
        gate = _sigmoid(gl_ref[:, sl].astype(F32) + bg_ref[:, sl])
        term = gate * _dot(y_ref[...], wb_ref[i])
        merged = term if merged is None else merged + term
    o_ref[...] = x_ref[...] + _dot(merged.astype(BF16), wo_ref[...])


def _merge(x2d, y_pool, y_dn, y_sb, proj2, b_gate, w_branch, w_out, *, tm=512):
    t, d = x2d.shape
    nb, bw, _ = w_branch.shape
    assert t % tm == 0
    vmem = (4 * tm * d * 4 + 6 * tm * bw * 2 + 2 * tm * nb * d * 2 + nb * bw * d * 2 + d * d * 2
            + 6 * tm * d * 4 + (8 << 20))
    y_spec = pl.BlockSpec((tm, bw), lambda i: (i, 0))
    return pl.pallas_call(
        _merge_kernel,
        grid=(t // tm,),
        in_specs=[
            pl.BlockSpec((tm, d), lambda i: (i, 0)),
            y_spec, y_spec, y_spec,
            pl.BlockSpec((tm, nb * d), lambda i: (i, 0)),
            _resident((1, nb * d)),
            _resident((nb, bw, d)),
            _resident((d, d)),
        ],
        out_specs=pl.BlockSpec((tm, d), lambda i: (i, 0)),
        out_shape=jax.ShapeDtypeStruct((t, d), F32),
        compiler_params=_cparams(("parallel",), vmem),
        name="merge",
    )(x2d, y_pool, y_dn, y_sb, proj2, b_gate, w_branch, w_out)


def kernel(x, ffn_norm, ffn_w_gate, ffn_w_up, ffn_w_down, mix_norm, w_in, b_gate, pool_w, pool_scale, dn_conv,
           dn_A_log, dn_dt_bias, dn_out_norm, w_branch, w_out, final_norm):
    bsz, seq, d = x.shape
    depth = w_in.shape[0]
    n_branch, bw = w_branch.shape[1], w_branch.shape[2]
    heads = bw // HEAD_DIM
    t = bsz * seq

    splits = (bw, 3 * bw, bw, heads, heads, 3 * bw, n_branch * d)
    offs = [0]
    for w in splits:
        offs.append(offs[-1] + w)
    o_pool, o_dnqkv, o_z, o_a, o_b, o_sb, o_gate, o_end = offs
    assert o_end == w_in.shape[2]
    c_gate, c_dnqkv = 0, n_branch * d
    c_sb = c_dnqkv + 3 * bw
    c_pool = c_sb + 3 * bw
    c_z = c_pool + bw
    n_main = c_z + bw

    x2 = x.reshape(t, d)
    for l in range(depth):
        wl = w_in[l]
        w_main = jnp.concatenate(
            [wl[:, o_gate:o_end], wl[:, o_dnqkv:o_z], wl[:, o_sb:o_gate], wl[:, o_pool:o_dnqkv], wl[:, o_z:o_a]],
            axis=1).astype(BF16)
        w_ab = jnp.pad(wl[:, o_a:o_sb], ((0, 0), (0, LANES - 2 * heads))).astype(BF16)
        row = lambda v: v.reshape(1, -1)

        x2 = _ffn(x2, row(ffn_norm[l, 0]), ffn_w_gate[l, 0].astype(BF16), ffn_w_up[l, 0].astype(BF16),
                  ffn_w_down[l, 0].astype(BF16), row(final_norm), final_norm=False)

        proj, ab = _proj(x2, row(mix_norm[l]), w_main, w_ab,
                         q_cols=(c_sb, c_sb + bw), q_scale=HEAD_DIM ** -0.5)
        proj3 = proj.reshape(bsz, seq, n_main)
        ab3 = ab.reshape(bsz, seq, LANES)
        abt3 = jnp.swapaxes(ab3[:, :, :SUBLANES], 1, 2)

        y_pool = _pool(proj3, pool_w[l].astype(BF16), row(pool_scale[l]), col_block=c_pool // bw)

        pad_lanes = LANES - heads
        prow = jnp.stack([jnp.pad(dn_A_log[l], (0, pad_lanes)), jnp.pad(dn_dt_bias[l], (0, pad_lanes))])
        pcol = jnp.pad(jnp.stack([dn_A_log[l], dn_dt_bias[l]], axis=1), ((0, SUBLANES - heads), (0, 0)))
        y_dn = _deltanet(proj3, ab3, abt3, dn_conv[l], prow, pcol, row(dn_out_norm[l]), heads=heads,
                         qkv_block=c_dnqkv // (3 * bw), z_block=c_z // bw)

        y_sb = _stickbreak(proj3, heads=heads, q_block=c_sb // HEAD_DIM)

        x2 = _merge(x2, y_pool.reshape(t, bw), y_dn.reshape(t, bw), y_sb.reshape(t, bw), proj,
                    row(b_gate[l]), w_branch[l].astype(BF16), w_out[l].astype(BF16))

        x2 = _ffn(x2, row(ffn_norm[l, 1]), ffn_w_gate[l, 1].astype(BF16), ffn_w_up[l, 1].astype(BF16),
                  ffn_w_down[l, 1].astype(BF16), row(final_norm), final_norm=(l == depth - 1))
    return x2.reshape(bsz, seq, d)
```

```python
import functools

import jax
import jax.numpy as jnp
from jax import lax
from jax.experimental import pallas as pl
from jax.experimental.pallas import tpu as pltpu

F32 = jnp.float32
BF16 = jnp.bfloat16

EPS = 1e-6
POOL_WINDOWS = (2, 4, 8, 16)
HEAD_DIM = 128
DN_CONV = 4
SUBLANES = 8
LANES = 128
V7X_VMEM_BYTES = 64 * 1024 * 1024

NT_DIMS = (((1,), (1,)), ((), ()))
TN_DIMS = (((0,), (0,)), ((), ()))


def _cparams(semantics, vmem_bytes):
    assert vmem_bytes < V7X_VMEM_BYTES
    return pltpu.CompilerParams(dimension_semantics=semantics, vmem_limit_bytes=int(vmem_bytes))


def _resident(shape):
    zeros = (0,) * len(shape)
    return pl.BlockSpec(shape, lambda *_: zeros, pipeline_mode=pl.Buffered(1))


def _sigmoid(x):
    return 1.0 / (1.0 + jnp.exp(-x))


def _softplus(x):
    return jnp.maximum(x, 0.0) + jnp.log(1.0 + jnp.exp(-jnp.abs(x)))


def _rms_norm(x, gain):
    return x * lax.rsqrt(jnp.mean(x * x, axis=-1, keepdims=True) + EPS) * gain


def _dot(a, b):
    return jnp.dot(a, b, preferred_element_type=F32)


def _split_bf16(x):
    hi = x.astype(BF16)
    lo = (x - hi.astype(F32)).astype(BF16)
    return hi, lo


def _ffn_kernel(x_ref, gain_ref, wg_ref, wu_ref, wd_ref, fgain_ref, o_ref, act_ref, *, f_chunk, final_norm):
    x = x_ref[...]
    h = _rms_norm(x, gain_ref[...]).astype(BF16)
    d_ff = wg_ref.shape[1]
    for c in range(d_ff // f_chunk):
        sl = slice(c * f_chunk, (c + 1) * f_chunk)
        g = _dot(h, wg_ref[:, sl])
        u = _dot(h, wu_ref[:, sl])
        act_ref[:, sl] = (g * _sigmoid(g) * u).astype(BF16)
    y = x + 0.5 * _dot(act_ref[...], wd_ref[...])
    if final_norm:
        y = _rms_norm(y, fgain_ref[...])
    o_ref[...] = y


def _ffn(x2d, gain, wg, wu, wd, final_gain, *, final_norm, tm=512, f_chunk=256):
    t, d = x2d.shape
    d_ff = wg.shape[1]
    assert t % tm == 0 and d_ff % f_chunk == 0
    vmem = (4 * tm * d * 4 + 3 * d * d_ff * 2 + tm * d_ff * 2 + 4 * tm * f_chunk * 4 + 2 * tm * d * 4) + (8 << 20)
    return pl.pallas_call(
        functools.partial(_ffn_kernel, f_chunk=f_chunk, final_norm=final_norm),
        grid=(t // tm,),
        in_specs=[
            pl.BlockSpec((tm, d), lambda i: (i, 0)),
            _resident((1, d)),
            _resident((d, d_ff)),
            _resident((d, d_ff)),
            _resident((d_ff, d)),
            _resident((1, d)),
        ],
        out_specs=pl.BlockSpec((tm, d), lambda i: (i, 0)),
        out_shape=jax.ShapeDtypeStruct((t, d), F32),
        scratch_shapes=[pltpu.VMEM((tm, d_ff), BF16)],
        compiler_params=_cparams(("parallel",), vmem),
        name="ffn",
    )(x2d, gain, wg, wu, wd, final_gain)


def _proj_kernel(x_ref, gain_ref, w_ref, wab_ref, o_ref, ab_ref, *, n_chunk, q_cols, q_scale):
    h = _rms_norm(x_ref[...], gain_ref[...]).astype(BF16)
    n_main = w_ref.shape[1]
    for c in range(n_main // n_chunk):
        lo, hi = c * n_chunk, (c + 1) * n_chunk
        p = _dot(h, w_ref[:, lo:hi])
        if q_cols[0] <= lo and hi <= q_cols[1]:
            p = p * q_scale
        else:
            assert hi <= q_cols[0] or lo >= q_cols[1]
        o_ref[:, lo:hi] = p.astype(BF16)
    ab_ref[...] = _dot(h, wab_ref[...])


def _proj(x2d, gain, w_main, w_ab, *, q_cols, q_scale, tm=512, n_chunk=512):
    t, d = x2d.shape
    n_main = w_main.shape[1]
    assert t % tm == 0 and n_main % n_chunk == 0
    vmem = 2 * tm * d * 4 + d * n_main * 2 + d * LANES * 2 + 2 * tm * n_main * 2 + 2 * tm * LANES * 4 + (8 << 20)
    return pl.pallas_call(
        functools.partial(_proj_kernel, n_chunk=n_chunk, q_cols=q_cols, q_scale=q_scale),
        grid=(t // tm,),
        in_specs=[
            pl.BlockSpec((tm, d), lambda i: (i, 0)),
            _resident((1, d)),
            _resident((d, n_main)),
            _resident((d, LANES)),
        ],
        out_specs=[
            pl.BlockSpec((tm, n_main), lambda i: (i, 0)),
            pl.BlockSpec((tm, LANES), lambda i: (i, 0)),
        ],
        out_shape=[
            jax.ShapeDtypeStruct((t, n_main), BF16),
            jax.ShapeDtypeStruct((t, LANES), F32),
        ],
        compiler_params=_cparams(("parallel",), vmem),
        name="in_proj",
    )(x2d, gain, w_main, w_ab)


def _shift_rows(x, s, row):
    return jnp.where(row < s, 0.0, pltpu.roll(x, s, 0))


def _pool_kernel(u_ref, w_ref, scale_ref, o_ref):
    seq = u_ref.shape[0]
    gdim = w_ref.shape[1]
    row = lax.broadcasted_iota(jnp.int32, (seq, gdim), 0)
    pos = (row + 1).astype(F32)
    for gi, win in enumerate(POOL_WINDOWS):
        sl = slice(gi * gdim, (gi + 1) * gdim)
        u = u_ref[:, sl].astype(F32)
        total = u
        span = 1
        while span < win:
            total = total + _shift_rows(total, span, row)
            span *= 2
        pooled = total / jnp.minimum(pos, float(win)) - u
        mixed = _dot(pooled.astype(BF16), w_ref[gi])
        o_ref[:, sl] = (mixed * scale_ref[:, sl]).astype(BF16)


def _pool(proj3, pool_w, scale, *, col_block):
    b, s, _ = proj3.shape
    g, gdim, _ = pool_w.shape
    width = g * gdim
    vmem = 4 * s * width * 2 + 12 * s * gdim * 4 + (8 << 20)
    return pl.pallas_call(
        _pool_kernel,
        grid=(b,),
        in_specs=[
            pl.BlockSpec((None, s, width), lambda i: (i, 0, col_block)),
            _resident((g, gdim, gdim)),
            _resident((1, width)),
        ],
        out_specs=pl.BlockSpec((None, s, width), lambda i: (i, 0, 0)),
        out_shape=jax.ShapeDtypeStruct((b, s, width), BF16),
        compiler_params=_cparams(("parallel",), vmem),
        name="pool",
    )(proj3, pool_w, scale)


def _row_sums(x):
    ones = jnp.ones((2 * x.shape[1], LANES), BF16)
    return _dot(jnp.concatenate(_split_bf16(x), axis=1), ones)


def _dn_kernel(qkv_ref, z_ref, ab_ref, abt_ref, conv_ref, prow_ref, pcol_ref, gain_ref, o_ref,
               state_ref, halo_ref, *, heads):
    nb, c = qkv_ref.shape[0], qkv_ref.shape[1]
    hd = HEAD_DIM
    width = heads * hd
    halo = SUBLANES

    @pl.when(pl.program_id(1) == 0)
    def _():
        state_ref[...] = jnp.zeros_like(state_ref)
        halo_ref[...] = jnp.zeros_like(halo_ref)

    ri = lax.broadcasted_iota(jnp.int32, (c, c), 0)
    ci = lax.broadcasted_iota(jnp.int32, (c, c), 1)
    incl = ri >= ci
    strict = ri > ci
    lower_ones = jnp.where(incl, 1.0, 0.0).astype(BF16)
    upper_ones = jnp.where(ri <= ci, 1.0, 0.0).astype(BF16)
    lower_ones2 = jnp.concatenate([lower_ones, lower_ones], axis=1)
    upper_ones2 = jnp.concatenate([upper_ones, upper_ones], axis=0)
    eye = jnp.where(ri == ci, 1.0, 0.0)

    si = lax.broadcasted_iota(jnp.int32, ((DN_CONV - 1) * c, c), 0)
    sj = lax.broadcasted_iota(jnp.int32, ((DN_CONV - 1) * c, c), 1)
    shift_ones = jnp.zeros(si.shape, F32)
    for s in range(1, DN_CONV):
        shift_ones = jnp.where((si - (s - 1) * c == sj + s) & (si < s * c), 1.0, shift_ones)
    shift_ones = shift_ones.astype(BF16)
    row8 = lax.broadcasted_iota(jnp.int32, (halo, 3 * width), 0)

    qs, ks, vs, betas, gcols, grows, egcs, erests, elasts = [], [], [], [], [], [], [], [], []
    for bi in range(nb):
        x16 = qkv_ref[bi]
        x = x16.astype(F32)
        shifted = _dot(shift_ones, x16)
        prev = halo_ref[bi]
        acc = x * conv_ref[DN_CONV - 1:DN_CONV, :]
        top = jnp.zeros((halo, 3 * width), F32)
        for s in range(1, DN_CONV):
            tap = conv_ref[DN_CONV - 1 - s:DN_CONV - s, :]
            acc = acc + shifted[(s - 1) * c:s * c] * tap
            top = top + jnp.where(row8 < s, pltpu.roll(prev, s, 0), 0.0) * tap
        acc = jnp.concatenate([acc[:halo] + top, acc[halo:]], axis=0)
        halo_ref[bi] = x[c - halo:]
        qkv = acc * _sigmoid(acc)

        ab = ab_ref[bi]
        abt = abt_ref[bi]
        g_col = -jnp.exp(prow_ref[0:1, :]) * _softplus(ab + prow_ref[1:2, :])
        g_row = -jnp.exp(pcol_ref[:, 0:1]) * _softplus(abt + pcol_ref[:, 1:2])
        beta_col = _sigmoid(ab)
        gc_col = _dot(lower_ones2, jnp.concatenate(_split_bf16(g_col), axis=0))
        gc_row = _dot(jnp.concatenate(_split_bf16(g_row), axis=1), upper_ones2)
        gc_last = gc_col[c - 1:c, :]
        egc_col = jnp.exp(gc_col)
        erest_col = jnp.exp(gc_last - gc_col)
        elast = jnp.exp(gc_last)
        for h in range(heads):
            qs.append(qkv[:, h * hd:(h + 1) * hd])
            ks.append(qkv[:, width + h * hd:width + (h + 1) * hd])
            vs.append(qkv[:, 2 * width + h * hd:2 * width + (h + 1) * hd])
            betas.append(beta_col[:, heads + h:heads + h + 1])
            gcols.append(gc_col[:, h:h + 1])
            grows.append(gc_row[h:h + 1, :])
            egcs.append(egc_col[:, h:h + 1])
            erests.append(erest_col[:, h:h + 1])
            elasts.append(elast[:, h:h + 1])
    groups = range(nb * heads)

    qs = [q * lax.rsqrt(_row_sums(q * q) + EPS) * (hd ** -0.5) for q in qs]
    ks = [k * lax.rsqrt(_row_sums(k * k) + EPS) for k in ks]
    kbs = [k * b for k, b in zip(ks, betas)]
    k16 = [k.astype(BF16) for k in ks]
    decays = [jnp.where(incl, jnp.exp(jnp.where(incl, gcol - grow, 0.0)), 0.0) for gcol, grow in zip(gcols, grows)]
    lmats = [lax.dot_general(kb.astype(BF16), k, NT_DIMS, preferred_element_type=F32) * jnp.where(strict, dec, 0.0)
             for kb, k, dec in zip(kbs, k16, decays)]
    attns = [(lax.dot_general(q.astype(BF16), k, NT_DIMS, preferred_element_type=F32) * dec).astype(BF16)
             for q, k, dec in zip(qs, k16, decays)]

    invs = [eye - l for l in lmats]
    pw16 = [l.astype(BF16) for l in lmats]
    span = 1
    while span < c // 2:
        pw16 = [_dot(p, p).astype(BF16) for p in pw16]
        invs = [x + _dot(x.astype(BF16), p) for x, p in zip(invs, pw16)]
        span *= 2
    rhs = [jnp.concatenate([v * b, kb * e], axis=1).astype(BF16) for v, b, kb, e in zip(vs, betas, kbs, egcs)]
    sols = [_dot(x.astype(BF16), r) for x, r in zip(invs, rhs)]

    states = [state_ref[g] for g in groups]
    s16 = [s.astype(BF16) for s in states]
    vn16 = [(sol[:, :hd] - _dot(sol[:, hd:].astype(BF16), s)).astype(BF16) for sol, s in zip(sols, s16)]
    outs = [_dot((q * e).astype(BF16), s) + _dot(a, vn) for q, e, s, a, vn in zip(qs, egcs, s16, attns, vn16)]
    for g in groups:
        k_dec = (ks[g] * erests[g]).astype(BF16)
        state_ref[g] = states[g] * elasts[g] + lax.dot_general(k_dec, vn16[g], TN_DIMS, preferred_element_type=F32)

    outs = [o * lax.rsqrt(_row_sums(o * o) * (1.0 / hd) + EPS) * gain_ref[...] for o in outs]
    for g in groups:
        bi, h = divmod(g, heads)
        zh = z_ref[bi, :, h * hd:(h + 1) * hd].astype(F32)
        o_ref[bi, :, h * hd:(h + 1) * hd] = (outs[g] * (zh * _sigmoid(zh))).astype(BF16)


def _deltanet(proj3, ab3, abt3, conv_w, prow, pcol, gain, *, heads, qkv_block, z_block, chunk=128, nb=4):
    b, s, _ = proj3.shape
    width = heads * HEAD_DIM
    assert s % chunk == 0 and b % nb == 0
    vmem = 4 * nb * chunk * 5 * width * 2 + nb * heads * 16 * chunk * HEAD_DIM * 4 + (8 << 20)
    return pl.pallas_call(
        functools.partial(_dn_kernel, heads=heads),
        grid=(b // nb, s // chunk),
        in_specs=[
            pl.BlockSpec((nb, chunk, 3 * width), lambda i, n: (i, n, qkv_block)),
            pl.BlockSpec((nb, chunk, width), lambda i, n: (i, n, z_block)),
            pl.BlockSpec((nb, chunk, LANES), lambda i, n: (i, n, 0)),
            pl.BlockSpec((nb, SUBLANES, chunk), lambda i, n: (i, 0, n)),
            _resident((DN_CONV, 3 * width)),
            _resident((2, LANES)),
            _resident((SUBLANES, 2)),
            _resident((1, HEAD_DIM)),
        ],
        out_specs=pl.BlockSpec((nb, chunk, width), lambda i, n: (i, n, 0)),
        out_shape=jax.ShapeDtypeStruct((b, s, width), BF16),
        scratch_shapes=[
            pltpu.VMEM((nb * heads, HEAD_DIM, HEAD_DIM), F32),
            pltpu.VMEM((nb, SUBLANES, 3 * width), F32),
        ],
        compiler_params=_cparams(("parallel", "arbitrary"), vmem),
        name="deltanet",
    )(proj3, proj3, ab3, abt3, conv_w, prow, pcol, gain)


def _sb_blocks(qs, ks, vs, runs, accs, upper, causal):
    zs = [lax.dot_general(q, k, NT_DIMS, preferred_element_type=F32) for q, k in zip(qs, ks)]
    sps = [_softplus(z) for z in zs]
    masked = sps if causal is None else [jnp.where(causal, sp, 0.0) for sp in sps]
    tails = [_dot(m.astype(BF16), upper) + run for m, run in zip(masked, runs)]
    wgts = [jnp.exp(z - sp - tail) for z, sp, tail in zip(zs, sps, tails)]
    if causal is not None:
        wgts = [jnp.where(causal, w, 0.0) for w in wgts]
    accs = [acc + _dot(w.astype(BF16), v) for acc, w, v in zip(accs, wgts, vs)]
    runs = [run + jnp.sum(m, axis=-1, keepdims=True) for run, m in zip(runs, masked)]
    return runs, accs


def _sb_kernel(q_ref, k_ref, v_ref, o_ref, *, heads, tk):
    tq = q_ref.shape[0]
    hd = HEAD_DIM
    assert tq == tk
    qi = pl.program_id(1)
    ri = lax.broadcasted_iota(jnp.int32, (tk, tk), 0)
    ci = lax.broadcasted_iota(jnp.int32, (tk, tk), 1)
    upper = jnp.where(ri > ci, 1.0, 0.0).astype(BF16)
    qs = [q_ref[:, h * hd:(h + 1) * hd] for h in range(heads)]

    def block(blk, runs, accs, causal):
        r0 = pl.multiple_of(blk * tk, tk)
        ks = [k_ref[pl.ds(r0, tk), h * hd:(h + 1) * hd] for h in range(heads)]
        vs = [v_ref[pl.ds(r0, tk), h * hd:(h + 1) * hd] for h in range(heads)]
        return _sb_blocks(qs, ks, vs, runs, accs, upper, causal)

    runs = [jnp.zeros((tq, 1), F32)] * heads
    accs = [jnp.zeros((tq, hd), F32)] * heads
    carry = block(qi, runs, accs, ci < ri)

    def body(i, carry):
        runs, accs = block(qi - 1 - i, list(carry[0]), list(carry[1]), None)
        return tuple(runs), tuple(accs)

    _, accs = lax.fori_loop(0, qi, body, (tuple(carry[0]), tuple(carry[1])))
    for h in range(heads):
        o_ref[:, h * hd:(h + 1) * hd] = accs[h].astype(BF16)


def _stickbreak(proj3, *, heads, q_block, tq=256):
    b, s, _ = proj3.shape
    width = heads * HEAD_DIM
    assert s % tq == 0 and q_block % heads == 0
    vmem = 4 * s * width * 2 + heads * 24 * tq * tq * 4 + (8 << 20)
    return pl.pallas_call(
        functools.partial(_sb_kernel, heads=heads, tk=tq),
        grid=(b, s // tq),
        in_specs=[
            pl.BlockSpec((None, tq, width), lambda i, t: (i, t, q_block // heads)),
            pl.BlockSpec((None, s, width), lambda i, t: (i, 0, q_block // heads + 1)),
            pl.BlockSpec((None, s, width), lambda i, t: (i, 0, q_block // heads + 2)),
        ],
        out_specs=pl.BlockSpec((None, tq, width), lambda i, t: (i, t, 0)),
        out_shape=jax.ShapeDtypeStruct((b, s, width), BF16),
        compiler_params=_cparams(("parallel", "arbitrary"), vmem),
        name="stickbreak",
    )(proj3, proj3, proj3)


def _merge_kernel(x_ref, yp_ref, yd_ref, ys_ref, gl_ref, bg_ref, wb_ref, wo_ref, o_ref):
    d = x_ref.shape[1]
    merged = None
    for i, y_ref in enumerate((yp_ref, yd_ref, ys_ref)):
        sl = slice(i * d, (i + 1) * d)
        gate = _sigmoid(gl_ref[:, sl].astype(F32) + bg_ref[:, sl])
        term = gate * _dot(y_ref[...], wb_ref[i])
        merged = term if merged is None else merged + term
    o_ref[...] = x_ref[...] + _dot(merged.astype(BF16), wo_ref[...])


def _merge(x2d, y_pool, y_dn, y_sb, proj2, b_gate, w_branch, w_out, *, tm=512):
    t, d = x2d.shape
    nb, bw, _ = w_branch.shape
    assert t % tm == 0
    vmem = (4 * tm * d * 4 + 6 * tm * bw * 2 + 2 * tm * nb * d * 2 + nb * bw * d * 2 + d * d * 2
            + 6 * tm * d * 4 + (8 << 20))
    y_spec = pl.BlockSpec((tm, bw), lambda i: (i, 0))
    return pl.pallas_call(
        _merge_kernel,
        grid=(t // tm,),
        in_specs=[
            pl.BlockSpec((tm, d), lambda i: (i, 0)),
            y_spec, y_spec, y_spec,
            pl.BlockSpec((tm, nb * d), lambda i: (i, 0)),
            _resident((1, nb * d)),
            _resident((nb, bw, d)),
            _resident((d, d)),
        ],
        out_specs=pl.BlockSpec((tm, d), lambda i: (i, 0)),
        out_shape=jax.ShapeDtypeStruct((t, d), F32),
        compiler_params=_cparams(("parallel",), vmem),
        name="merge",
    )(x2d, y_pool, y_dn, y_sb, proj2, b_gate, w_branch, w_out)


def kernel(x, ffn_norm, ffn_w_gate, ffn_w_up, ffn_w_down, mix_norm, w_in, b_gate, pool_w, pool_scale, dn_conv,
           dn_A_log, dn_dt_bias, dn_out_norm, w_branch, w_out, final_norm):
    bsz, seq, d = x.shape
    depth = w_in.shape[0]
    n_branch, bw = w_branch.shape[1], w_branch.shape[2]
    heads = bw // HEAD_DIM
    t = bsz * seq

    splits = (bw, 3 * bw, bw, heads, heads, 3 * bw, n_branch * d)
    offs = [0]
    for w in splits:
        offs.append(offs[-1] + w)
    o_pool, o_dnqkv, o_z, o_a, o_b, o_sb, o_gate, o_end = offs
    assert o_end == w_in.shape[2]
    c_gate, c_dnqkv = 0, n_branch * d
    c_sb = c_dnqkv + 3 * bw
    c_pool = c_sb + 3 * bw
    c_z = c_pool + bw
    n_main = c_z + bw

    x2 = x.reshape(t, d)
    for l in range(depth):
        wl = w_in[l]
        w_main = jnp.concatenate(
            [wl[:, o_gate:o_end], wl[:, o_dnqkv:o_z], wl[:, o_sb:o_gate], wl[:, o_pool:o_dnqkv], wl[:, o_z:o_a]],
            axis=1).astype(BF16)
        w_ab = jnp.pad(wl[:, o_a:o_sb], ((0, 0), (0, LANES - 2 * heads))).astype(BF16)
        row = lambda v: v.reshape(1, -1)

        x2 = _ffn(x2, row(ffn_norm[l, 0]), ffn_w_gate[l, 0].astype(BF16), ffn_w_up[l, 0].astype(BF16),
                  ffn_w_down[l, 0].astype(BF16), row(final_norm), final_norm=False)

        proj, ab = _proj(x2, row(mix_norm[l]), w_main, w_ab,
                         q_cols=(c_sb, c_sb + bw), q_scale=HEAD_DIM ** -0.5)
        proj3 = proj.reshape(bsz, seq, n_main)
        ab3 = ab.reshape(bsz, seq, LANES)
        abt3 = jnp.swapaxes(ab3[:, :, :SUBLANES], 1, 2)

        y_pool = _pool(proj3, pool_w[l].astype(BF16), row(pool_scale[l]), col_block=c_pool // bw)

        pad_lanes = LANES - heads
        prow = jnp.stack([jnp.pad(dn_A_log[l], (0, pad_lanes)), jnp.pad(dn_dt_bias[l], (0, pad_lanes))])
        pcol = jnp.pad(jnp.stack([dn_A_log[l], dn_dt_bias[l]], axis=1), ((0, SUBLANES - heads), (0, 0)))
        y_dn = _deltanet(proj3, ab3, abt3, dn_conv[l], prow, pcol, row(dn_out_norm[l]), heads=heads,
                         qkv_block=c_dnqkv // (3 * bw), z_block=c_z // bw)

        y_sb = _stickbreak(proj3, heads=heads, q_block=c_sb // HEAD_DIM)

        x2 = _merge(x2, y_pool.reshape(t, bw), y_dn.reshape(t, bw), y_sb.reshape(t, bw), proj,
                    row(b_gate[l]), w_branch[l].astype(BF16), w_out[l].astype(BF16))

        x2 = _ffn(x2, row(ffn_norm[l, 1]), ffn_w_gate[l, 1].astype(BF16), ffn_w_up[l, 1].astype(BF16),
                  ffn_w_down[l, 1].astype(BF16), row(final_norm), final_norm=(l == depth - 1))
    return x2.reshape(bsz, seq, d)
```

```python
import functools

import jax
import jax.numpy as jnp
from jax import lax
from jax.experimental import pallas as pl
from jax.experimental.pallas import tpu as pltpu

F32 = jnp.float32
BF16 = jnp.bfloat16

EPS = 1e-6
POOL_WINDOWS = (2, 4, 8, 16)
HEAD_DIM = 128
DN_CONV = 4
SUBLANES = 8
LANES = 128
V7X_VMEM_BYTES = 64 * 1024 * 1024

NT_DIMS = (((1,), (1,)), ((), ()))
TN_DIMS = (((0,), (0,)), ((), ()))


def _cparams(semantics, vmem_bytes):
    assert vmem_bytes < V7X_VMEM_BYTES
    return pltpu.CompilerParams(dimension_semantics=semantics, vmem_limit_bytes=int(vmem_bytes))


def _resident(shape):
    zeros = (0,) * len(shape)
    return pl.BlockSpec(shape, lambda *_: zeros, pipeline_mode=pl.Buffered(1))


def _sigmoid(x):
    return 1.0 / (1.0 + jnp.exp(-x))


def _softplus(x):
    return jnp.maximum(x, 0.0) + jnp.log(1.0 + jnp.exp(-jnp.abs(x)))


def _rms_norm(x, gain):
    return x * lax.rsqrt(jnp.mean(x * x, axis=-1, keepdims=True) + EPS) * gain


def _dot(a, b):
    return jnp.dot(a, b, preferred_element_type=F32)


def _split_bf16(x):
    hi = x.astype(BF16)
    lo = (x - hi.astype(F32)).astype(BF16)
    return hi, lo


def _ffn_kernel(x_ref, gain_ref, wg_ref, wu_ref, wd_ref, fgain_ref, o_ref, act_ref, *, f_chunk, final_norm):
    x = x_ref[...]
    h = _rms_norm(x, gain_ref[...]).astype(BF16)
    d_ff = wg_ref.shape[1]
    for c in range(d_ff // f_chunk):
        sl = slice(c * f_chunk, (c + 1) * f_chunk)
        g = _dot(h, wg_ref[:, sl])
        u = _dot(h, wu_ref[:, sl])
        act_ref[:, sl] = (g * _sigmoid(g) * u).astype(BF16)
    y = x + 0.5 * _dot(act_ref[...], wd_ref[...])
    if final_norm:
        y = _rms_norm(y, fgain_ref[...])
    o_ref[...] = y


def _ffn(x2d, gain, wg, wu, wd, final_gain, *, final_norm, tm=512, f_chunk=256):
    t, d = x2d.shape
    d_ff = wg.shape[1]
    assert t % tm == 0 and d_ff % f_chunk == 0
    vmem = (4 * tm * d * 4 + 3 * d * d_ff * 2 + tm * d_ff * 2 + 4 * tm * f_chunk * 4 + 2 * tm * d * 4) + (8 << 20)
    return pl.pallas_call(
        functools.partial(_ffn_kernel, f_chunk=f_chunk, final_norm=final_norm),
        grid=(t // tm,),
        in_specs=[
            pl.BlockSpec((tm, d), lambda i: (i, 0)),
            _resident((1, d)),
            _resident((d, d_ff)),
            _resident((d, d_ff)),
            _resident((d_ff, d)),
            _resident((1, d)),
        ],
        out_specs=pl.BlockSpec((tm, d), lambda i: (i, 0)),
        out_shape=jax.ShapeDtypeStruct((t, d), F32),
        scratch_shapes=[pltpu.VMEM((tm, d_ff), BF16)],
        compiler_params=_cparams(("parallel",), vmem),
        name="ffn",
    )(x2d, gain, wg, wu, wd, final_gain)


def _proj_kernel(x_ref, gain_ref, w_ref, wab_ref, o_ref, ab_ref, *, n_chunk, q_cols, q_scale):
    h = _rms_norm(x_ref[...], gain_ref[...]).astype(BF16)
    n_main = w_ref.shape[1]
    for c in range(n_main // n_chunk):
        lo, hi = c * n_chunk, (c + 1) * n_chunk
        p = _dot(h, w_ref[:, lo:hi])
        if q_cols[0] <= lo and hi <= q_cols[1]:
            p = p * q_scale
        else:
            assert hi <= q_cols[0] or lo >= q_cols[1]
        o_ref[:, lo:hi] = p.astype(BF16)
    ab_ref[...] = _dot(h, wab_ref[...])


def _proj(x2d, gain, w_main, w_ab, *, q_cols, q_scale, tm=512, n_chunk=512):
    t, d = x2d.shape
    n_main = w_main.shape[1]
    assert t % tm == 0 and n_main % n_chunk == 0
    vmem = 2 * tm * d * 4 + d * n_main * 2 + d * LANES * 2 + 2 * tm * n_main * 2 + 2 * tm * LANES * 4 + (8 << 20)
    return pl.pallas_call(
        functools.partial(_proj_kernel, n_chunk=n_chunk, q_cols=q_cols, q_scale=q_scale),
        grid=(t // tm,),
        in_specs=[
            pl.BlockSpec((tm, d), lambda i: (i, 0)),
            _resident((1, d)),
            _resident((d, n_main)),
            _resident((d, LANES)),
        ],
        out_specs=[
            pl.BlockSpec((tm, n_main), lambda i: (i, 0)),
            pl.BlockSpec((tm, LANES), lambda i: (i, 0)),
        ],
        out_shape=[
            jax.ShapeDtypeStruct((t, n_main), BF16),
            jax.ShapeDtypeStruct((t, LANES), F32),
        ],
        compiler_params=_cparams(("parallel",), vmem),
        name="in_proj",
    )(x2d, gain, w_main, w_ab)


def _shift_rows(x, s, row):
    return jnp.where(row < s, 0.0, pltpu.roll(x, s, 0))


def _pool_kernel(u_ref, w_ref, scale_ref, o_ref):
    seq = u_ref.shape[0]
    gdim = w_ref.shape[1]
    row = lax.broadcasted_iota(jnp.int32, (seq, gdim), 0)
    pos = (row + 1).astype(F32)
    for gi, win in enumerate(POOL_WINDOWS):
        sl = slice(gi * gdim, (gi + 1) * gdim)
        u = u_ref[:, sl].astype(F32)
        total = u
        span = 1
        while span < win:
            total = total + _shift_rows(total, span, row)
            span *= 2
        pooled = total / jnp.minimum(pos, float(win)) - u
        mixed = _dot(pooled.astype(BF16), w_ref[gi])
        o_ref[:, sl] = (mixed * scale_ref[:, sl]).astype(BF16)


def _pool(proj3, pool_w, scale, *, col_block):
    b, s, _ = proj3.shape
    g, gdim, _ = pool_w.shape
    width = g * gdim
    vmem = 4 * s * width * 2 + 12 * s * gdim * 4 + (8 << 20)
    return pl.pallas_call(
        _pool_kernel,
        grid=(b,),
        in_specs=[
            pl.BlockSpec((None, s, width), lambda i: (i, 0, col_block)),
            _resident((g, gdim, gdim)),
            _resident((1, width)),
        ],
        out_specs=pl.BlockSpec((None, s, width), lambda i: (i, 0, 0)),
        out_shape=jax.ShapeDtypeStruct((b, s, width), BF16),
        compiler_params=_cparams(("parallel",), vmem),
        name="pool",
    )(proj3, pool_w, scale)


def _row_sums(x):
    ones = jnp.ones((2 * x.shape[1], LANES), BF16)
    return _dot(jnp.concatenate(_split_bf16(x), axis=1), ones)


def _dn_kernel(qkv_ref, z_ref, ab_ref, abt_ref, conv_ref, prow_ref, pcol_ref, gain_ref, o_ref,
               state_ref, halo_ref, *, heads):
    nb, c = qkv_ref.shape[0], qkv_ref.shape[1]
    hd = HEAD_DIM
    width = heads * hd
    halo = SUBLANES

    @pl.when(pl.program_id(1) == 0)
    def _():
        state_ref[...] = jnp.zeros_like(state_ref)
        halo_ref[...] = jnp.zeros_like(halo_ref)

    ri = lax.broadcasted_iota(jnp.int32, (c, c), 0)
    ci = lax.broadcasted_iota(jnp.int32, (c, c), 1)
    incl = ri >= ci
    strict = ri > ci
    lower_ones = jnp.where(incl, 1.0, 0.0).astype(BF16)
    upper_ones = jnp.where(ri <= ci, 1.0, 0.0).astype(BF16)
    lower_ones2 = jnp.concatenate([lower_ones, lower_ones], axis=1)
    upper_ones2 = jnp.concatenate([upper_ones, upper_ones], axis=0)
    eye = jnp.where(ri == ci, 1.0, 0.0)

    si = lax.broadcasted_iota(jnp.int32, ((DN_CONV - 1) * c, c), 0)
    sj = lax.broadcasted_iota(jnp.int32, ((DN_CONV - 1) * c, c), 1)
    shift_ones = jnp.zeros(si.shape, F32)
    for s in range(1, DN_CONV):
        shift_ones = jnp.where((si - (s - 1) * c == sj + s) & (si < s * c), 1.0, shift_ones)
    shift_ones = shift_ones.astype(BF16)
    row8 = lax.broadcasted_iota(jnp.int32, (halo, 3 * width), 0)

    qs, ks, vs, betas, gcols, grows, egcs, erests, elasts = [], [], [], [], [], [], [], [], []
    for bi in range(nb):
        x16 = qkv_ref[bi]
        x = x16.astype(F32)
        shifted = _dot(shift_ones, x16)
        prev = halo_ref[bi]
        acc = x * conv_ref[DN_CONV - 1:DN_CONV, :]
        top = jnp.zeros((halo, 3 * width), F32)
        for s in range(1, DN_CONV):
            tap = conv_ref[DN_CONV - 1 - s:DN_CONV - s, :]
            acc = acc + shifted[(s - 1) * c:s * c] * tap
            top = top + jnp.where(row8 < s, pltpu.roll(prev, s, 0), 0.0) * tap
        acc = jnp.concatenate([acc[:halo] + top, acc[halo:]], axis=0)
        halo_ref[bi] = x[c - halo:]
        qkv = acc * _sigmoid(acc)

        ab = ab_ref[bi]
        abt = abt_ref[bi]
        g_col = -jnp.exp(prow_ref[0:1, :]) * _softplus(ab + prow_ref[1:2, :])
        g_row = -jnp.exp(pcol_ref[:, 0:1]) * _softplus(abt + pcol_ref[:, 1:2])
        beta_col = _sigmoid(ab)
        gc_col = _dot(lower_ones2, jnp.concatenate(_split_bf16(g_col), axis=0))
        gc_row = _dot(jnp.concatenate(_split_bf16(g_row), axis=1), upper_ones2)
        gc_last = gc_col[c - 1:c, :]
        egc_col = jnp.exp(gc_col)
        erest_col = jnp.exp(gc_last - gc_col)
        elast = jnp.exp(gc_last)
        for h in range(heads):
            qs.append(qkv[:, h * hd:(h + 1) * hd])
            ks.append(qkv[:, width + h * hd:width + (h + 1) * hd])
            vs.append(qkv[:, 2 * width + h * hd:2 * width + (h + 1) * hd])
            betas.append(beta_col[:, heads + h:heads + h + 1])
            gcols.append(gc_col[:, h:h + 1])
            grows.append(gc_row[h:h + 1, :])
            egcs.append(egc_col[:, h:h + 1])
            erests.append(erest_col[:, h:h + 1])
            elasts.append(elast[:, h:h + 1])
    groups = range(nb * heads)

    qs = [q * lax.rsqrt(_row_sums(q * q) + EPS) * (hd ** -0.5) for q in qs]
    ks = [k * lax.rsqrt(_row_sums(k * k) + EPS) for k in ks]
    kbs = [k * b for k, b in zip(ks, betas)]
    k16 = [k.astype(BF16) for k in ks]
    decays = [jnp.where(incl, jnp.exp(jnp.where(incl, gcol - grow, 0.0)), 0.0) for gcol, grow in zip(gcols, grows)]
    lmats = [lax.dot_general(kb.astype(BF16), k, NT_DIMS, preferred_element_type=F32) * jnp.where(strict, dec, 0.0)
             for kb, k, dec in zip(kbs, k16, decays)]
    attns = [(lax.dot_general(q.astype(BF16), k, NT_DIMS, preferred_element_type=F32) * dec).astype(BF16)
             for q, k, dec in zip(qs, k16, decays)]

    invs = [eye - l for l in lmats]
    pw16 = [l.astype(BF16) for l in lmats]
    span = 1
    while span < c // 2:
        pw16 = [_dot(p, p).astype(BF16) for p in pw16]
        invs = [x + _dot(x.astype(BF16), p) for x, p in zip(invs, pw16)]
        span *= 2
    rhs = [jnp.concatenate([v * b, kb * e], axis=1).astype(BF16) for v, b, kb, e in zip(vs, betas, kbs, egcs)]
    sols = [_dot(x.astype(BF16), r) for x, r in zip(invs, rhs)]

    states = [state_ref[g] for g in groups]
    s16 = [s.astype(BF16) for s in states]
    vn16 = [(sol[:, :hd] - _dot(sol[:, hd:].astype(BF16), s)).astype(BF16) for sol, s in zip(sols, s16)]
    outs = [_dot((q * e).astype(BF16), s) + _dot(a, vn) for q, e, s, a, vn in zip(qs, egcs, s16, attns, vn16)]
    for g in groups:
        k_dec = (ks[g] * erests[g]).astype(BF16)
        state_ref[g] = states[g] * elasts[g] + lax.dot_general(k_dec, vn16[g], TN_DIMS, preferred_element_type=F32)

    outs = [o * lax.rsqrt(_row_sums(o * o) * (1.0 / hd) + EPS) * gain_ref[...] for o in outs]
    for g in groups:
        bi, h = divmod(g, heads)
        zh = z_ref[bi, :, h * hd:(h + 1) * hd].astype(F32)
        o_ref[bi, :, h * hd:(h + 1) * hd] = (outs[g] * (zh * _sigmoid(zh))).astype(BF16)


def _deltanet(proj3, ab3, abt3, conv_w, prow, pcol, gain, *, heads, qkv_block, z_block, chunk=128, nb=4):
    b, s, _ = proj3.shape
    width = heads * HEAD_DIM
    assert s % chunk == 0 and b % nb == 0
    vmem = 4 * nb * chunk * 5 * width * 2 + nb * heads * 16 * chunk * HEAD_DIM * 4 + (8 << 20)
    return pl.pallas_call(
        functools.partial(_dn_kernel, heads=heads),
        grid=(b // nb, s // chunk),
        in_specs=[
            pl.BlockSpec((nb, chunk, 3 * width), lambda i, n: (i, n, qkv_block)),
            pl.BlockSpec((nb, chunk, width), lambda i, n: (i, n, z_block)),
            pl.BlockSpec((nb, chunk, LANES), lambda i, n: (i, n, 0)),
            pl.BlockSpec((nb, SUBLANES, chunk), lambda i, n: (i, 0, n)),
            _resident((DN_CONV, 3 * width)),
            _resident((2, LANES)),
            _resident((SUBLANES, 2)),
            _resident((1, HEAD_DIM)),
        ],
        out_specs=pl.BlockSpec((nb, chunk, width), lambda i, n: (i, n, 0)),
        out_shape=jax.ShapeDtypeStruct((b, s, width), BF16),
        scratch_shapes=[
            pltpu.VMEM((nb * heads, HEAD_DIM, HEAD_DIM), F32),
            pltpu.VMEM((nb, SUBLANES, 3 * width), F32),
        ],
        compiler_params=_cparams(("parallel", "arbitrary"), vmem),
        name="deltanet",
    )(proj3, proj3, ab3, abt3, conv_w, prow, pcol, gain)


SB_UNDERFLOW = 105.0


def _sb_blocks(qs, ks, vs, runs, accs, upper, causal):
    zs = [lax.dot_general(q, k, NT_DIMS, preferred_element_type=F32) for q, k in zip(qs, ks)]
    sps = [_softplus(z) for z in zs]
    masked = sps if causal is None else [jnp.where(causal, sp, 0.0) for sp in sps]
    tails = [_dot(m.astype(BF16), upper) + run for m, run in zip(masked, runs)]
    wgts = [jnp.exp(z - sp - tail) for z, sp, tail in zip(zs, sps, tails)]
    if causal is not None:
        wgts = [jnp.where(causal, w, 0.0) for w in wgts]
    accs = [acc + _dot(w.astype(BF16), v) for acc, w, v in zip(accs, wgts, vs)]
    runs = [run + jnp.sum(m, axis=-1, keepdims=True) for run, m in zip(runs, masked)]
    return runs, accs


def _sb_kernel(q_ref, k_ref, v_ref, o_ref, *, heads, tk):
    tq = q_ref.shape[0]
    hd = HEAD_DIM
    assert tq == tk
    qi = pl.program_id(1)
    ri = lax.broadcasted_iota(jnp.int32, (tk, tk), 0)
    ci = lax.broadcasted_iota(jnp.int32, (tk, tk), 1)
    upper = jnp.where(ri > ci, 1.0, 0.0).astype(BF16)
    qs = [q_ref[:, h * hd:(h + 1) * hd] for h in range(heads)]

    def block(blk, runs, accs, causal):
        r0 = pl.multiple_of(blk * tk, tk)
        ks = [k_ref[pl.ds(r0, tk), h * hd:(h + 1) * hd] for h in range(heads)]
        vs = [v_ref[pl.ds(r0, tk), h * hd:(h + 1) * hd] for h in range(heads)]
        return _sb_blocks(qs, ks, vs, runs, accs, upper, causal)

    runs = [jnp.zeros((tq, 1), F32)] * heads
    accs = [jnp.zeros((tq, hd), F32)] * heads
    runs, accs = block(qi, runs, accs, ci < ri)

    def least(runs):
        return jnp.min(functools.reduce(jnp.minimum, runs))

    def more(carry):
        return (carry[0] < qi) & (carry[1] < SB_UNDERFLOW)

    def body(carry):
        i, _, runs, accs = carry
        runs, accs = block(qi - 1 - i, list(runs), list(accs), None)
        return i + 1, least(runs), tuple(runs), tuple(accs)

    _, _, _, accs = lax.while_loop(more, body, (jnp.int32(0), least(runs), tuple(runs), tuple(accs)))
    for h in range(heads):
        o_ref[:, h * hd:(h + 1) * hd] = accs[h].astype(BF16)


def _stickbreak(proj3, *, heads, q_block, tq=256):
    b, s, _ = proj3.shape
    width = heads * HEAD_DIM
    assert s % tq == 0 and q_block % heads == 0
    vmem = 4 * s * width * 2 + heads * 24 * tq * tq * 4 + (8 << 20)
    return pl.pallas_call(
        functools.partial(_sb_kernel, heads=heads, tk=tq),
        grid=(b, s // tq),
        in_specs=[
            pl.BlockSpec((None, tq, width), lambda i, t: (i, t, q_block // heads)),
            pl.BlockSpec((None, s, width), lambda i, t: (i, 0, q_block // heads + 1)),
            pl.BlockSpec((None, s, width), lambda i, t: (i, 0, q_block // heads + 2)),
        ],
        out_specs=pl.BlockSpec((None, tq, width), lambda i, t: (i, t, 0)),
        out_shape=jax.ShapeDtypeStruct((b, s, width), BF16),
        compiler_params=_cparams(("parallel", "arbitrary"), vmem),
        name="stickbreak",
    )(proj3, proj3, proj3)


def _merge_kernel(x_ref, yp_ref, yd_ref, ys_ref, gl_ref, bg_ref, wb_ref, wo_ref, o_ref):
    d = x_ref.shape[1]
    merged = None
    for i, y_ref in enumerate((yp_ref, yd_ref, ys_ref)):
        sl = slice(i * d, (i + 1) * d)
        gate = _sigmoid(gl_ref[:, sl].astype(F32) + bg_ref[:, sl])
        term = gate * _dot(y_ref[...], wb_ref[i])
        merged = term if merged is None else merged + term
    o_ref[...] = x_ref[...] + _dot(merged.astype(BF16), wo_ref[...])


def _merge(x2d, y_pool, y_dn, y_sb, proj2, b_gate, w_branch, w_out, *, tm=512):
    t, d = x2d.shape
    nb, bw, _ = w_branch.shape
    assert t % tm == 0
    vmem = (4 * tm * d * 4 + 6 * tm * bw * 2 + 2 * tm * nb * d * 2 + nb * bw * d * 2 + d * d * 2
            + 6 * tm * d * 4 + (8 << 20))
    y_spec = pl.BlockSpec((tm, bw), lambda i: (i, 0))
    return pl.pallas_call(
        _merge_kernel,
        grid=(t // tm,),
        in_specs=[
            pl.BlockSpec((tm, d), lambda i: (i, 0)),
            y_spec, y_spec, y_spec,
            pl.BlockSpec((tm, nb * d), lambda i: (i, 0)),
            _resident((1, nb * d)),
            _resident((nb, bw, d)),
            _resident((d, d)),
        ],
        out_specs=pl.BlockSpec((tm, d), lambda i: (i, 0)),
        out_shape=jax.ShapeDtypeStruct((t, d), F32),
        compiler_params=_cparams(("parallel",), vmem),
        name="merge",
    )(x2d, y_pool, y_dn, y_sb, proj2, b_gate, w_branch, w_out)


def kernel(x, ffn_norm, ffn_w_gate, ffn_w_up, ffn_w_down, mix_norm, w_in, b_gate, pool_w, pool_scale, dn_conv,
           dn_A_log, dn_dt_bias, dn_out_norm, w_branch, w_out, final_norm):
    bsz, seq, d = x.shape
    depth = w_in.shape[0]
    n_branch, bw = w_branch.shape[1], w_branch.shape[2]
    heads = bw // HEAD_DIM
    t = bsz * seq

    splits = (bw, 3 * bw, bw, heads, heads, 3 * bw, n_branch * d)
    offs = [0]
    for w in splits:
        offs.append(offs[-1] + w)
    o_pool, o_dnqkv, o_z, o_a, o_b, o_sb, o_gate, o_end = offs
    assert o_end == w_in.shape[2]
    c_gate, c_dnqkv = 0, n_branch * d
    c_sb = c_dnqkv + 3 * bw
    c_pool = c_sb + 3 * bw
    c_z = c_pool + bw
    n_main = c_z + bw

    x2 = x.reshape(t, d)
    for l in range(depth):
        wl = w_in[l]
        w_main = jnp.concatenate(
            [wl[:, o_gate:o_end], wl[:, o_dnqkv:o_z], wl[:, o_sb:o_gate], wl[:, o_pool:o_dnqkv], wl[:, o_z:o_a]],
            axis=1).astype(BF16)
        w_ab = jnp.pad(wl[:, o_a:o_sb], ((0, 0), (0, LANES - 2 * heads))).astype(BF16)
        row = lambda v: v.reshape(1, -1)

        x2 = _ffn(x2, row(ffn_norm[l, 0]), ffn_w_gate[l, 0].astype(BF16), ffn_w_up[l, 0].astype(BF16),
                  ffn_w_down[l, 0].astype(BF16), row(final_norm), final_norm=False)

        proj, ab = _proj(x2, row(mix_norm[l]), w_main, w_ab,
                         q_cols=(c_sb, c_sb + bw), q_scale=HEAD_DIM ** -0.5)
        proj3 = proj.reshape(bsz, seq, n_main)
        ab3 = ab.reshape(bsz, seq, LANES)
        abt3 = jnp.swapaxes(ab3[:, :, :SUBLANES], 1, 2)

        y_pool = _pool(proj3, pool_w[l].astype(BF16), row(pool_scale[l]), col_block=c_pool // bw)

        pad_lanes = LANES - heads
        prow = jnp.stack([jnp.pad(dn_A_log[l], (0, pad_lanes)), jnp.pad(dn_dt_bias[l], (0, pad_lanes))])
        pcol = jnp.pad(jnp.stack([dn_A_log[l], dn_dt_bias[l]], axis=1), ((0, SUBLANES - heads), (0, 0)))
        y_dn = _deltanet(proj3, ab3, abt3, dn_conv[l], prow, pcol, row(dn_out_norm[l]), heads=heads,
                         qkv_block=c_dnqkv // (3 * bw), z_block=c_z // bw)

        y_sb = _stickbreak(proj3, heads=heads, q_block=c_sb // HEAD_DIM)

        x2 = _merge(x2, y_pool.reshape(t, bw), y_dn.reshape(t, bw), y_sb.reshape(t, bw), proj,
                    row(b_gate[l]), w_branch[l].astype(BF16), w_out[l].astype(BF16))

        x2 = _ffn(x2, row(ffn_norm[l, 1]), ffn_w_gate[l, 1].astype(BF16), ffn_w_up[l, 1].astype(BF16),
                  ffn_w_down[l, 1].astype(BF16), row(final_norm), final_norm=(l == depth - 1))
    return x2.reshape(bsz, seq, d)
```

```python
import functools

import jax
import jax.numpy as jnp
from jax import lax
from jax.experimental import pallas as pl
from jax.experimental.pallas import tpu as pltpu

F32 = jnp.float32
BF16 = jnp.bfloat16

EPS = 1e-6
POOL_WINDOWS = (2, 4, 8, 16)
HEAD_DIM = 128
DN_CONV = 4
SUBLANES = 8
LANES = 128
V7X_VMEM_BYTES = 64 * 1024 * 1024

NT_DIMS = (((1,), (1,)), ((), ()))
TN_DIMS = (((0,), (0,)), ((), ()))


def _cparams(semantics, vmem_bytes):
    assert vmem_bytes < V7X_VMEM_BYTES
    return pltpu.CompilerParams(dimension_semantics=semantics, vmem_limit_bytes=int(vmem_bytes))


def _resident(shape):
    zeros = (0,) * len(shape)
    return pl.BlockSpec(shape, lambda *_: zeros, pipeline_mode=pl.Buffered(1))


def _sigmoid(x):
    return 1.0 / (1.0 + jnp.exp(-x))


def _softplus(x):
    return jnp.maximum(x, 0.0) + jnp.log(1.0 + jnp.exp(-jnp.abs(x)))


def _rms_norm(x, gain):
    return x * lax.rsqrt(jnp.mean(x * x, axis=-1, keepdims=True) + EPS) * gain


def _dot(a, b):
    return jnp.dot(a, b, preferred_element_type=F32)


def _split_bf16(x):
    hi = x.astype(BF16)
    lo = (x - hi.astype(F32)).astype(BF16)
    return hi, lo


def _ffn_half_step(x, gain_ref, wg_ref, wu_ref, wd_ref, fgain_ref, act_ref, f_chunk, final_norm):
    h = _rms_norm(x, gain_ref[...]).astype(BF16)
    d_ff = wg_ref.shape[1]
    for c in range(d_ff // f_chunk):
        sl = slice(c * f_chunk, (c + 1) * f_chunk)
        g = _dot(h, wg_ref[:, sl])
        u = _dot(h, wu_ref[:, sl])
        act_ref[:, sl] = (g * _sigmoid(g) * u).astype(BF16)
    y = x + 0.5 * _dot(act_ref[...], wd_ref[...])
    if final_norm:
        y = _rms_norm(y, fgain_ref[...])
    return y


def _ffn_kernel(x_ref, gain_ref, wg_ref, wu_ref, wd_ref, fgain_ref, o_ref, act_ref, *, f_chunk, final_norm):
    o_ref[...] = _ffn_half_step(x_ref[...], gain_ref, wg_ref, wu_ref, wd_ref, fgain_ref, act_ref, f_chunk, final_norm)


def _ffn(x2d, gain, wg, wu, wd, final_gain, *, final_norm, tm=512, f_chunk=256):
    t, d = x2d.shape
    d_ff = wg.shape[1]
    assert t % tm == 0 and d_ff % f_chunk == 0
    vmem = (4 * tm * d * 4 + 3 * d * d_ff * 2 + tm * d_ff * 2 + 4 * tm * f_chunk * 4 + 2 * tm * d * 4) + (8 << 20)
    return pl.pallas_call(
        functools.partial(_ffn_kernel, f_chunk=f_chunk, final_norm=final_norm),
        grid=(t // tm,),
        in_specs=[
            pl.BlockSpec((tm, d), lambda i: (i, 0)),
            _resident((1, d)),
            _resident((d, d_ff)),
            _resident((d, d_ff)),
            _resident((d_ff, d)),
            _resident((1, d)),
        ],
        out_specs=pl.BlockSpec((tm, d), lambda i: (i, 0)),
        out_shape=jax.ShapeDtypeStruct((t, d), F32),
        scratch_shapes=[pltpu.VMEM((tm, d_ff), BF16)],
        compiler_params=_cparams(("parallel",), vmem),
        name="ffn",
    )(x2d, gain, wg, wu, wd, final_gain)


def _proj_kernel(x_ref, gain_ref, w_ref, wab_ref, o_ref, ab_ref, *, n_chunk, q_cols, q_scale):
    h = _rms_norm(x_ref[...], gain_ref[...]).astype(BF16)
    n_main = w_ref.shape[1]
    for c in range(n_main // n_chunk):
        lo, hi = c * n_chunk, (c + 1) * n_chunk
        p = _dot(h, w_ref[:, lo:hi])
        if q_cols[0] <= lo and hi <= q_cols[1]:
            p = p * q_scale
        else:
            assert hi <= q_cols[0] or lo >= q_cols[1]
        o_ref[:, lo:hi] = p.astype(BF16)
    ab_ref[...] = _dot(h, wab_ref[...])


def _proj(x2d, gain, w_main, w_ab, *, q_cols, q_scale, tm=512, n_chunk=512):
    t, d = x2d.shape
    n_main = w_main.shape[1]
    assert t % tm == 0 and n_main % n_chunk == 0
    vmem = 2 * tm * d * 4 + d * n_main * 2 + d * LANES * 2 + 2 * tm * n_main * 2 + 2 * tm * LANES * 4 + (8 << 20)
    return pl.pallas_call(
        functools.partial(_proj_kernel, n_chunk=n_chunk, q_cols=q_cols, q_scale=q_scale),
        grid=(t // tm,),
        in_specs=[
            pl.BlockSpec((tm, d), lambda i: (i, 0)),
            _resident((1, d)),
            _resident((d, n_main)),
            _resident((d, LANES)),
        ],
        out_specs=[
            pl.BlockSpec((tm, n_main), lambda i: (i, 0)),
            pl.BlockSpec((tm, LANES), lambda i: (i, 0)),
        ],
        out_shape=[
            jax.ShapeDtypeStruct((t, n_main), BF16),
            jax.ShapeDtypeStruct((t, LANES), F32),
        ],
        compiler_params=_cparams(("parallel",), vmem),
        name="in_proj",
    )(x2d, gain, w_main, w_ab)


def _shift_rows(x, s, row):
    return jnp.where(row < s, 0.0, pltpu.roll(x, s, 0))


def _pool_kernel(u_ref, w_ref, scale_ref, o_ref):
    seq = u_ref.shape[0]
    gdim = w_ref.shape[1]
    row = lax.broadcasted_iota(jnp.int32, (seq, gdim), 0)
    pos = (row + 1).astype(F32)
    for gi, win in enumerate(POOL_WINDOWS):
        sl = slice(gi * gdim, (gi + 1) * gdim)
        u = u_ref[:, sl].astype(F32)
        total = u
        span = 1
        while span < win:
            total = total + _shift_rows(total, span, row)
            span *= 2
        pooled = total / jnp.minimum(pos, float(win)) - u
        mixed = _dot(pooled.astype(BF16), w_ref[gi])
        o_ref[:, sl] = (mixed * scale_ref[:, sl]).astype(BF16)


def _pool(proj3, pool_w, scale, *, col_block):
    b, s, _ = proj3.shape
    g, gdim, _ = pool_w.shape
    width = g * gdim
    vmem = 4 * s * width * 2 + 12 * s * gdim * 4 + (8 << 20)
    return pl.pallas_call(
        _pool_kernel,
        grid=(b,),
        in_specs=[
            pl.BlockSpec((None, s, width), lambda i: (i, 0, col_block)),
            _resident((g, gdim, gdim)),
            _resident((1, width)),
        ],
        out_specs=pl.BlockSpec((None, s, width), lambda i: (i, 0, 0)),
        out_shape=jax.ShapeDtypeStruct((b, s, width), BF16),
        compiler_params=_cparams(("parallel",), vmem),
        name="pool",
    )(proj3, pool_w, scale)


def _row_sums(x):
    ones = jnp.ones((2 * x.shape[1], LANES), BF16)
    return _dot(jnp.concatenate(_split_bf16(x), axis=1), ones)


def _dn_kernel(qkv_ref, z_ref, ab_ref, abt_ref, conv_ref, prow_ref, pcol_ref, gain_ref, o_ref,
               state_ref, halo_ref, *, heads):
    nb, c = qkv_ref.shape[0], qkv_ref.shape[1]
    hd = HEAD_DIM
    width = heads * hd
    halo = SUBLANES

    @pl.when(pl.program_id(1) == 0)
    def _():
        state_ref[...] = jnp.zeros_like(state_ref)
        halo_ref[...] = jnp.zeros_like(halo_ref)

    ri = lax.broadcasted_iota(jnp.int32, (c, c), 0)
    ci = lax.broadcasted_iota(jnp.int32, (c, c), 1)
    incl = ri >= ci
    strict = ri > ci
    lower_ones = jnp.where(incl, 1.0, 0.0).astype(BF16)
    upper_ones = jnp.where(ri <= ci, 1.0, 0.0).astype(BF16)
    lower_ones2 = jnp.concatenate([lower_ones, lower_ones], axis=1)
    upper_ones2 = jnp.concatenate([upper_ones, upper_ones], axis=0)
    eye = jnp.where(ri == ci, 1.0, 0.0)

    si = lax.broadcasted_iota(jnp.int32, ((DN_CONV - 1) * c, c), 0)
    sj = lax.broadcasted_iota(jnp.int32, ((DN_CONV - 1) * c, c), 1)
    shift_ones = jnp.zeros(si.shape, F32)
    for s in range(1, DN_CONV):
        shift_ones = jnp.where((si - (s - 1) * c == sj + s) & (si < s * c), 1.0, shift_ones)
    shift_ones = shift_ones.astype(BF16)
    row8 = lax.broadcasted_iota(jnp.int32, (halo, 3 * width), 0)

    qs, ks, vs, betas, gcols, grows, egcs, erests, elasts = [], [], [], [], [], [], [], [], []
    for bi in range(nb):
        x16 = qkv_ref[bi]
        x = x16.astype(F32)
        shifted = _dot(shift_ones, x16)
        prev = halo_ref[bi]
        acc = x * conv_ref[DN_CONV - 1:DN_CONV, :]
        top = jnp.zeros((halo, 3 * width), F32)
        for s in range(1, DN_CONV):
            tap = conv_ref[DN_CONV - 1 - s:DN_CONV - s, :]
            acc = acc + shifted[(s - 1) * c:s * c] * tap
            top = top + jnp.where(row8 < s, pltpu.roll(prev, s, 0), 0.0) * tap
        acc = jnp.concatenate([acc[:halo] + top, acc[halo:]], axis=0)
        halo_ref[bi] = x[c - halo:]
        qkv = acc * _sigmoid(acc)

        ab = ab_ref[bi]
        abt = abt_ref[bi]
        g_col = -jnp.exp(prow_ref[0:1, :]) * _softplus(ab + prow_ref[1:2, :])
        g_row = -jnp.exp(pcol_ref[:, 0:1]) * _softplus(abt + pcol_ref[:, 1:2])
        beta_col = _sigmoid(ab)
        gc_col = _dot(lower_ones2, jnp.concatenate(_split_bf16(g_col), axis=0))
        gc_row = _dot(jnp.concatenate(_split_bf16(g_row), axis=1), upper_ones2)
        gc_last = gc_col[c - 1:c, :]
        egc_col = jnp.exp(gc_col)
        erest_col = jnp.exp(gc_last - gc_col)
        elast = jnp.exp(gc_last)
        for h in range(heads):
            qs.append(qkv[:, h * hd:(h + 1) * hd])
            ks.append(qkv[:, width + h * hd:width + (h + 1) * hd])
            vs.append(qkv[:, 2 * width + h * hd:2 * width + (h + 1) * hd])
            betas.append(beta_col[:, heads + h:heads + h + 1])
            gcols.append(gc_col[:, h:h + 1])
            grows.append(gc_row[h:h + 1, :])
            egcs.append(egc_col[:, h:h + 1])
            erests.append(erest_col[:, h:h + 1])
            elasts.append(elast[:, h:h + 1])
    groups = range(nb * heads)

    qs = [q * lax.rsqrt(_row_sums(q * q) + EPS) * (hd ** -0.5) for q in qs]
    ks = [k * lax.rsqrt(_row_sums(k * k) + EPS) for k in ks]
    kbs = [k * b for k, b in zip(ks, betas)]
    k16 = [k.astype(BF16) for k in ks]
    decays = [jnp.where(incl, jnp.exp(jnp.where(incl, gcol - grow, 0.0)), 0.0) for gcol, grow in zip(gcols, grows)]
    lmats = [lax.dot_general(kb.astype(BF16), k, NT_DIMS, preferred_element_type=F32) * jnp.where(strict, dec, 0.0)
             for kb, k, dec in zip(kbs, k16, decays)]
    attns = [(lax.dot_general(q.astype(BF16), k, NT_DIMS, preferred_element_type=F32) * dec).astype(BF16)
             for q, k, dec in zip(qs, k16, decays)]

    invs = [eye - l for l in lmats]
    pw16 = [l.astype(BF16) for l in lmats]
    span = 1
    while span < c // 2:
        pw16 = [_dot(p, p).astype(BF16) for p in pw16]
        invs = [x + _dot(x.astype(BF16), p) for x, p in zip(invs, pw16)]
        span *= 2
    rhs = [jnp.concatenate([v * b, kb * e], axis=1).astype(BF16) for v, b, kb, e in zip(vs, betas, kbs, egcs)]
    sols = [_dot(x.astype(BF16), r) for x, r in zip(invs, rhs)]

    states = [state_ref[g] for g in groups]
    s16 = [s.astype(BF16) for s in states]
    vn16 = [(sol[:, :hd] - _dot(sol[:, hd:].astype(BF16), s)).astype(BF16) for sol, s in zip(sols, s16)]
    outs = [_dot((q * e).astype(BF16), s) + _dot(a, vn) for q, e, s, a, vn in zip(qs, egcs, s16, attns, vn16)]
    for g in groups:
        k_dec = (ks[g] * erests[g]).astype(BF16)
        state_ref[g] = states[g] * elasts[g] + lax.dot_general(k_dec, vn16[g], TN_DIMS, preferred_element_type=F32)

    outs = [o * lax.rsqrt(_row_sums(o * o) * (1.0 / hd) + EPS) * gain_ref[...] for o in outs]
    for g in groups:
        bi, h = divmod(g, heads)
        zh = z_ref[bi, :, h * hd:(h + 1) * hd].astype(F32)
        o_ref[bi, :, h * hd:(h + 1) * hd] = (outs[g] * (zh * _sigmoid(zh))).astype(BF16)


def _deltanet(proj3, ab3, abt3, conv_w, prow, pcol, gain, *, heads, qkv_block, z_block, chunk=128, nb=4):
    b, s, _ = proj3.shape
    width = heads * HEAD_DIM
    assert s % chunk == 0 and b % nb == 0
    vmem = 4 * nb * chunk * 5 * width * 2 + nb * heads * 16 * chunk * HEAD_DIM * 4 + (8 << 20)
    return pl.pallas_call(
        functools.partial(_dn_kernel, heads=heads),
        grid=(b // nb, s // chunk),
        in_specs=[
            pl.BlockSpec((nb, chunk, 3 * width), lambda i, n: (i, n, qkv_block)),
            pl.BlockSpec((nb, chunk, width), lambda i, n: (i, n, z_block)),
            pl.BlockSpec((nb, chunk, LANES), lambda i, n: (i, n, 0)),
            pl.BlockSpec((nb, SUBLANES, chunk), lambda i, n: (i, 0, n)),
            _resident((DN_CONV, 3 * width)),
            _resident((2, LANES)),
            _resident((SUBLANES, 2)),
            _resident((1, HEAD_DIM)),
        ],
        out_specs=pl.BlockSpec((nb, chunk, width), lambda i, n: (i, n, 0)),
        out_shape=jax.ShapeDtypeStruct((b, s, width), BF16),
        scratch_shapes=[
            pltpu.VMEM((nb * heads, HEAD_DIM, HEAD_DIM), F32),
            pltpu.VMEM((nb, SUBLANES, 3 * width), F32),
        ],
        compiler_params=_cparams(("parallel", "arbitrary"), vmem),
        name="deltanet",
    )(proj3, proj3, ab3, abt3, conv_w, prow, pcol, gain)


SB_UNDERFLOW = 105.0


def _sb_blocks(qs, ks, vs, runs, accs, upper, causal):
    zs = [lax.dot_general(q, k, NT_DIMS, preferred_element_type=F32) for q, k in zip(qs, ks)]
    sps = [_softplus(z) for z in zs]
    masked = sps if causal is None else [jnp.where(causal, sp, 0.0) for sp in sps]
    tails = [_dot(m.astype(BF16), upper) + run for m, run in zip(masked, runs)]
    wgts = [jnp.exp(z - sp - tail) for z, sp, tail in zip(zs, sps, tails)]
    if causal is not None:
        wgts = [jnp.where(causal, w, 0.0) for w in wgts]
    accs = [acc + _dot(w.astype(BF16), v) for acc, w, v in zip(accs, wgts, vs)]
    runs = [run + jnp.sum(m, axis=-1, keepdims=True) for run, m in zip(runs, masked)]
    return runs, accs


def _sb_kernel(q_ref, k_ref, v_ref, o_ref, *, heads, tk):
    tq = q_ref.shape[0]
    hd = HEAD_DIM
    assert tq == tk
    qi = pl.program_id(1)
    ri = lax.broadcasted_iota(jnp.int32, (tk, tk), 0)
    ci = lax.broadcasted_iota(jnp.int32, (tk, tk), 1)
    upper = jnp.where(ri > ci, 1.0, 0.0).astype(BF16)
    qs = [q_ref[:, h * hd:(h + 1) * hd] for h in range(heads)]

    def block(blk, runs, accs, causal):
        r0 = pl.multiple_of(blk * tk, tk)
        ks = [k_ref[pl.ds(r0, tk), h * hd:(h + 1) * hd] for h in range(heads)]
        vs = [v_ref[pl.ds(r0, tk), h * hd:(h + 1) * hd] for h in range(heads)]
        return _sb_blocks(qs, ks, vs, runs, accs, upper, causal)

    runs = [jnp.zeros((tq, 1), F32)] * heads
    accs = [jnp.zeros((tq, hd), F32)] * heads
    runs, accs = block(qi, runs, accs, ci < ri)

    def least(runs):
        return jnp.min(functools.reduce(jnp.minimum, runs))

    def more(carry):
        return (carry[0] < qi) & (carry[1] < SB_UNDERFLOW)

    def body(carry):
        i, _, runs, accs = carry
        runs, accs = block(qi - 1 - i, list(runs), list(accs), None)
        return i + 1, least(runs), tuple(runs), tuple(accs)

    _, _, _, accs = lax.while_loop(more, body, (jnp.int32(0), least(runs), tuple(runs), tuple(accs)))
    for h in range(heads):
        o_ref[:, h * hd:(h + 1) * hd] = accs[h].astype(BF16)


def _stickbreak(proj3, *, heads, q_block, tq=256):
    b, s, _ = proj3.shape
    width = heads * HEAD_DIM
    assert s % tq == 0 and q_block % heads == 0
    vmem = 4 * s * width * 2 + heads * 24 * tq * tq * 4 + (8 << 20)
    return pl.pallas_call(
        functools.partial(_sb_kernel, heads=heads, tk=tq),
        grid=(b, s // tq),
        in_specs=[
            pl.BlockSpec((None, tq, width), lambda i, t: (i, t, q_block // heads)),
            pl.BlockSpec((None, s, width), lambda i, t: (i, 0, q_block // heads + 1)),
            pl.BlockSpec((None, s, width), lambda i, t: (i, 0, q_block // heads + 2)),
        ],
        out_specs=pl.BlockSpec((None, tq, width), lambda i, t: (i, t, 0)),
        out_shape=jax.ShapeDtypeStruct((b, s, width), BF16),
        compiler_params=_cparams(("parallel", "arbitrary"), vmem),
        name="stickbreak",
    )(proj3, proj3, proj3)


def _merge_ffn_kernel(x_ref, yp_ref, yd_ref, ys_ref, gl_ref, bg_ref, wb_ref, wo_ref, gain_ref, wg_ref, wu_ref, wd_ref,
                      fgain_ref, o_ref, act_ref, *, f_chunk, final_norm):
    d = x_ref.shape[1]
    merged = None
    for i, y_ref in enumerate((yp_ref, yd_ref, ys_ref)):
        sl = slice(i * d, (i + 1) * d)
        gate = _sigmoid(gl_ref[:, sl].astype(F32) + bg_ref[:, sl])
        term = gate * _dot(y_ref[...], wb_ref[i])
        merged = term if merged is None else merged + term
    x = x_ref[...] + _dot(merged.astype(BF16), wo_ref[...])
    o_ref[...] = _ffn_half_step(x, gain_ref, wg_ref, wu_ref, wd_ref, fgain_ref, act_ref, f_chunk, final_norm)


def _merge_ffn(x2d, y_pool, y_dn, y_sb, proj2, b_gate, w_branch, w_out, gain, wg, wu, wd, final_gain, *,
               final_norm, tm=512, f_chunk=256):
    t, d = x2d.shape
    nb, bw, _ = w_branch.shape
    d_ff = wg.shape[1]
    assert t % tm == 0 and d_ff % f_chunk == 0
    vmem = (4 * tm * d * 4 + 6 * tm * bw * 2 + 2 * tm * nb * d * 2 + nb * bw * d * 2 + d * d * 2 + 3 * d * d_ff * 2
            + tm * d_ff * 2 + 6 * tm * d * 4 + (6 << 20))
    y_spec = pl.BlockSpec((tm, bw), lambda i: (i, 0))
    return pl.pallas_call(
        functools.partial(_merge_ffn_kernel, f_chunk=f_chunk, final_norm=final_norm),
        grid=(t // tm,),
        in_specs=[
            pl.BlockSpec((tm, d), lambda i: (i, 0)),
            y_spec, y_spec, y_spec,
            pl.BlockSpec((tm, nb * d), lambda i: (i, 0)),
            _resident((1, nb * d)),
            _resident((nb, bw, d)),
            _resident((d, d)),
            _resident((1, d)),
            _resident((d, d_ff)),
            _resident((d, d_ff)),
            _resident((d_ff, d)),
            _resident((1, d)),
        ],
        out_specs=pl.BlockSpec((tm, d), lambda i: (i, 0)),
        out_shape=jax.ShapeDtypeStruct((t, d), F32),
        scratch_shapes=[pltpu.VMEM((tm, d_ff), BF16)],
        compiler_params=_cparams(("parallel",), vmem),
        name="merge_ffn",
    )(x2d, y_pool, y_dn, y_sb, proj2, b_gate, w_branch, w_out, gain, wg, wu, wd, final_gain)


def kernel(x, ffn_norm, ffn_w_gate, ffn_w_up, ffn_w_down, mix_norm, w_in, b_gate, pool_w, pool_scale, dn_conv,
           dn_A_log, dn_dt_bias, dn_out_norm, w_branch, w_out, final_norm):
    bsz, seq, d = x.shape
    depth = w_in.shape[0]
    n_branch, bw = w_branch.shape[1], w_branch.shape[2]
    heads = bw // HEAD_DIM
    t = bsz * seq

    splits = (bw, 3 * bw, bw, heads, heads, 3 * bw, n_branch * d)
    offs = [0]
    for w in splits:
        offs.append(offs[-1] + w)
    o_pool, o_dnqkv, o_z, o_a, o_b, o_sb, o_gate, o_end = offs
    assert o_end == w_in.shape[2]
    c_gate, c_dnqkv = 0, n_branch * d
    c_sb = c_dnqkv + 3 * bw
    c_pool = c_sb + 3 * bw
    c_z = c_pool + bw
    n_main = c_z + bw

    x2 = x.reshape(t, d)
    for l in range(depth):
        wl = w_in[l]
        w_main = jnp.concatenate(
            [wl[:, o_gate:o_end], wl[:, o_dnqkv:o_z], wl[:, o_sb:o_gate], wl[:, o_pool:o_dnqkv], wl[:, o_z:o_a]],
            axis=1).astype(BF16)
        w_ab = jnp.pad(wl[:, o_a:o_sb], ((0, 0), (0, LANES - 2 * heads))).astype(BF16)
        row = lambda v: v.reshape(1, -1)

        x2 = _ffn(x2, row(ffn_norm[l, 0]), ffn_w_gate[l, 0].astype(BF16), ffn_w_up[l, 0].astype(BF16),
                  ffn_w_down[l, 0].astype(BF16), row(final_norm), final_norm=False)

        proj, ab = _proj(x2, row(mix_norm[l]), w_main, w_ab,
                         q_cols=(c_sb, c_sb + bw), q_scale=HEAD_DIM ** -0.5)
        proj3 = proj.reshape(bsz, seq, n_main)
        ab3 = ab.reshape(bsz, seq, LANES)
        abt3 = jnp.swapaxes(ab3[:, :, :SUBLANES], 1, 2)

        y_pool = _pool(proj3, pool_w[l].astype(BF16), row(pool_scale[l]), col_block=c_pool // bw)

        pad_lanes = LANES - heads
        prow = jnp.stack([jnp.pad(dn_A_log[l], (0, pad_lanes)), jnp.pad(dn_dt_bias[l], (0, pad_lanes))])
        pcol = jnp.pad(jnp.stack([dn_A_log[l], dn_dt_bias[l]], axis=1), ((0, SUBLANES - heads), (0, 0)))
        y_dn = _deltanet(proj3, ab3, abt3, dn_conv[l], prow, pcol, row(dn_out_norm[l]), heads=heads,
                         qkv_block=c_dnqkv // (3 * bw), z_block=c_z // bw)

        y_sb = _stickbreak(proj3, heads=heads, q_block=c_sb // HEAD_DIM)

        x2 = _merge_ffn(x2, y_pool.reshape(t, bw), y_dn.reshape(t, bw), y_sb.reshape(t, bw), proj,
                        row(b_gate[l]), w_branch[l].astype(BF16), w_out[l].astype(BF16),
                        row(ffn_norm[l, 1]), ffn_w_gate[l, 1].astype(BF16), ffn_w_up[l, 1].astype(BF16),
                        ffn_w_down[l, 1].astype(BF16), row(final_norm), final_norm=(l == depth - 1))
    return x2.reshape(bsz, seq, d)
```

```python
import functools

import jax
import jax.numpy as jnp
from jax import lax
from jax.experimental import pallas as pl
from jax.experimental.pallas import tpu as pltpu

F32 = jnp.float32
BF16 = jnp.bfloat16

EPS = 1e-6
POOL_WINDOWS = (2, 4, 8, 16)
HEAD_DIM = 128
DN_CONV = 4
SUBLANES = 8
LANES = 128
V7X_VMEM_BYTES = 64 * 1024 * 1024

NT_DIMS = (((1,), (1,)), ((), ()))
TN_DIMS = (((0,), (0,)), ((), ()))


def _cparams(semantics, vmem_bytes):
    assert vmem_bytes < V7X_VMEM_BYTES
    return pltpu.CompilerParams(dimension_semantics=semantics, vmem_limit_bytes=int(vmem_bytes))


def _resident(shape):
    zeros = (0,) * len(shape)
    return pl.BlockSpec(shape, lambda *_: zeros, pipeline_mode=pl.Buffered(1))


def _resident_slice(shape, lead):
    tail = shape[len(lead):]
    index = tuple(lead) + (0,) * len(tail)
    return pl.BlockSpec((None,) * len(lead) + tuple(tail), lambda *_: index, pipeline_mode=pl.Buffered(1))


def _sigmoid(x):
    return 1.0 / (1.0 + jnp.exp(-x))


def _softplus(x):
    return jnp.maximum(x, 0.0) + jnp.log(1.0 + jnp.exp(-jnp.abs(x)))


def _rms_norm(x, gain):
    return x * lax.rsqrt(jnp.mean(x * x, axis=-1, keepdims=True) + EPS) * gain


def _dot(a, b):
    return jnp.dot(a, b, preferred_element_type=F32)


def _split_bf16(x):
    hi = x.astype(BF16)
    lo = (x - hi.astype(F32)).astype(BF16)
    return hi, lo


def _ffn_half_step(x, gain_ref, wg_ref, wu_ref, wd_ref, fgain_ref, act_ref, f_chunk, final_norm):
    h = _rms_norm(x, gain_ref[...]).astype(BF16)
    d_ff = wg_ref.shape[1]
    for c in range(d_ff // f_chunk):
        sl = slice(c * f_chunk, (c + 1) * f_chunk)
        g = _dot(h, wg_ref[:, sl])
        u = _dot(h, wu_ref[:, sl])
        act_ref[:, sl] = (g * _sigmoid(g) * u).astype(BF16)
    y = x + 0.5 * _dot(act_ref[...], wd_ref[...])
    if final_norm:
        y = _rms_norm(y, fgain_ref[...])
    return y


def _ffn_kernel(x_ref, gain_ref, wg_ref, wu_ref, wd_ref, fgain_ref, o_ref, act_ref, *, f_chunk, final_norm):
    o_ref[...] = _ffn_half_step(x_ref[...], gain_ref, wg_ref, wu_ref, wd_ref, fgain_ref, act_ref, f_chunk, final_norm)


def _ffn(x2d, gain, wg, wu, wd, final_gain, *, which, final_norm, tm=512, f_chunk=256):
    t, d = x2d.shape
    d_ff = wg.shape[-1]
    assert t % tm == 0 and d_ff % f_chunk == 0
    vmem = (4 * tm * d * 4 + 3 * d * d_ff * 2 + tm * d_ff * 2 + 4 * tm * f_chunk * 4 + 2 * tm * d * 4) + (8 << 20)
    return pl.pallas_call(
        functools.partial(_ffn_kernel, f_chunk=f_chunk, final_norm=final_norm),
        grid=(t // tm,),
        in_specs=[
            pl.BlockSpec((tm, d), lambda i: (i, 0)),
            _resident((1, d)),
            _resident_slice(wg.shape, which),
            _resident_slice(wu.shape, which),
            _resident_slice(wd.shape, which),
            _resident((1, d)),
        ],
        out_specs=pl.BlockSpec((tm, d), lambda i: (i, 0)),
        out_shape=jax.ShapeDtypeStruct((t, d), F32),
        scratch_shapes=[pltpu.VMEM((tm, d_ff), BF16)],
        compiler_params=_cparams(("parallel",), vmem),
        name="ffn",
    )(x2d, gain, wg, wu, wd, final_gain)


def _proj_kernel(x_ref, gain_ref, w_ref, wab_ref, o_ref, ab_ref, *, n_chunk, q_cols, q_scale):
    h = _rms_norm(x_ref[...], gain_ref[...]).astype(BF16)
    n_main = w_ref.shape[1]
    for c in range(n_main // n_chunk):
        lo, hi = c * n_chunk, (c + 1) * n_chunk
        p = _dot(h, w_ref[:, lo:hi])
        if q_cols[0] <= lo and hi <= q_cols[1]:
            p = p * q_scale
        else:
            assert hi <= q_cols[0] or lo >= q_cols[1]
        o_ref[:, lo:hi] = p.astype(BF16)
    ab_ref[...] = _dot(h, wab_ref[...])


def _proj(x2d, gain, w_main, w_ab, *, q_cols, q_scale, tm=512, n_chunk=512):
    t, d = x2d.shape
    n_main = w_main.shape[1]
    assert t % tm == 0 and n_main % n_chunk == 0
    vmem = 2 * tm * d * 4 + d * n_main * 2 + d * LANES * 2 + 2 * tm * n_main * 2 + 2 * tm * LANES * 4 + (8 << 20)
    return pl.pallas_call(
        functools.partial(_proj_kernel, n_chunk=n_chunk, q_cols=q_cols, q_scale=q_scale),
        grid=(t // tm,),
        in_specs=[
            pl.BlockSpec((tm, d), lambda i: (i, 0)),
            _resident((1, d)),
            _resident((d, n_main)),
            _resident((d, LANES)),
        ],
        out_specs=[
            pl.BlockSpec((tm, n_main), lambda i: (i, 0)),
            pl.BlockSpec((tm, LANES), lambda i: (i, 0)),
        ],
        out_shape=[
            jax.ShapeDtypeStruct((t, n_main), BF16),
            jax.ShapeDtypeStruct((t, LANES), F32),
        ],
        compiler_params=_cparams(("parallel",), vmem),
        name="in_proj",
    )(x2d, gain, w_main, w_ab)


def _shift_rows(x, s, row):
    return jnp.where(row < s, 0.0, pltpu.roll(x, s, 0))


def _pool_kernel(u_ref, w_ref, scale_ref, o_ref):
    seq = u_ref.shape[0]
    gdim = w_ref.shape[1]
    row = lax.broadcasted_iota(jnp.int32, (seq, gdim), 0)
    pos = (row + 1).astype(F32)
    for gi, win in enumerate(POOL_WINDOWS):
        sl = slice(gi * gdim, (gi + 1) * gdim)
        u = u_ref[:, sl].astype(F32)
        total = u
        span = 1
        while span < win:
            total = total + _shift_rows(total, span, row)
            span *= 2
        pooled = total / jnp.minimum(pos, float(win)) - u
        mixed = _dot(pooled.astype(BF16), w_ref[gi])
        o_ref[:, sl] = (mixed * scale_ref[:, sl]).astype(BF16)


def _pool(proj3, pool_w, scale, *, col_block):
    b, s, _ = proj3.shape
    g, gdim, _ = pool_w.shape
    width = g * gdim
    vmem = 4 * s * width * 2 + 12 * s * gdim * 4 + (8 << 20)
    return pl.pallas_call(
        _pool_kernel,
        grid=(b,),
        in_specs=[
            pl.BlockSpec((None, s, width), lambda i: (i, 0, col_block)),
            _resident((g, gdim, gdim)),
            _resident((1, width)),
        ],
        out_specs=pl.BlockSpec((None, s, width), lambda i: (i, 0, 0)),
        out_shape=jax.ShapeDtypeStruct((b, s, width), BF16),
        compiler_params=_cparams(("parallel",), vmem),
        name="pool",
    )(proj3, pool_w, scale)


def _row_sums(x):
    ones = jnp.ones((2 * x.shape[1], LANES), BF16)
    return _dot(jnp.concatenate(_split_bf16(x), axis=1), ones)


def _dn_kernel(qkv_ref, z_ref, ab_ref, abt_ref, conv_ref, prow_ref, pcol_ref, gain_ref, o_ref,
               state_ref, halo_ref, *, heads):
    nb, c = qkv_ref.shape[0], qkv_ref.shape[1]
    hd = HEAD_DIM
    width = heads * hd
    halo = SUBLANES

    @pl.when(pl.program_id(1) == 0)
    def _():
        state_ref[...] = jnp.zeros_like(state_ref)
        halo_ref[...] = jnp.zeros_like(halo_ref)

    ri = lax.broadcasted_iota(jnp.int32, (c, c), 0)
    ci = lax.broadcasted_iota(jnp.int32, (c, c), 1)
    incl = ri >= ci
    strict = ri > ci
    lower_ones = jnp.where(incl, 1.0, 0.0).astype(BF16)
    upper_ones = jnp.where(ri <= ci, 1.0, 0.0).astype(BF16)
    lower_ones2 = jnp.concatenate([lower_ones, lower_ones], axis=1)
    upper_ones2 = jnp.concatenate([upper_ones, upper_ones], axis=0)
    eye = jnp.where(ri == ci, 1.0, 0.0)

    si = lax.broadcasted_iota(jnp.int32, ((DN_CONV - 1) * c, c), 0)
    sj = lax.broadcasted_iota(jnp.int32, ((DN_CONV - 1) * c, c), 1)
    shift_ones = jnp.zeros(si.shape, F32)
    for s in range(1, DN_CONV):
        shift_ones = jnp.where((si - (s - 1) * c == sj + s) & (si < s * c), 1.0, shift_ones)
    shift_ones = shift_ones.astype(BF16)
    row8 = lax.broadcasted_iota(jnp.int32, (halo, 3 * width), 0)

    qs, ks, vs, betas, gcols, grows, egcs, erests, elasts = [], [], [], [], [], [], [], [], []
    for bi in range(nb):
        x16 = qkv_ref[bi]
        x = x16.astype(F32)
        shifted = _dot(shift_ones, x16)
        prev = halo_ref[bi]
        acc = x * conv_ref[DN_CONV - 1:DN_CONV, :]
        top = jnp.zeros((halo, 3 * width), F32)
        for s in range(1, DN_CONV):
            tap = conv_ref[DN_CONV - 1 - s:DN_CONV - s, :]
            acc = acc + shifted[(s - 1) * c:s * c] * tap
            top = top + jnp.where(row8 < s, pltpu.roll(prev, s, 0), 0.0) * tap
        acc = jnp.concatenate([acc[:halo] + top, acc[halo:]], axis=0)
        halo_ref[bi] = x[c - halo:]
        qkv = acc * _sigmoid(acc)

        ab = ab_ref[bi]
        abt = abt_ref[bi]
        g_col = -jnp.exp(prow_ref[0:1, :]) * _softplus(ab + prow_ref[1:2, :])
        g_row = -jnp.exp(pcol_ref[:, 0:1]) * _softplus(abt + pcol_ref[:, 1:2])
        beta_col = _sigmoid(ab)
        gc_col = _dot(lower_ones2, jnp.concatenate(_split_bf16(g_col), axis=0))
        gc_row = _dot(jnp.concatenate(_split_bf16(g_row), axis=1), upper_ones2)
        gc_last = gc_col[c - 1:c, :]
        egc_col = jnp.exp(gc_col)
        erest_col = jnp.exp(gc_last - gc_col)
        elast = jnp.exp(gc_last)
        for h in range(heads):
            qs.append(qkv[:, h * hd:(h + 1) * hd])
            ks.append(qkv[:, width + h * hd:width + (h + 1) * hd])
            vs.append(qkv[:, 2 * width + h * hd:2 * width + (h + 1) * hd])
            betas.append(beta_col[:, heads + h:heads + h + 1])
            gcols.append(gc_col[:, h:h + 1])
            grows.append(gc_row[h:h + 1, :])
            egcs.append(egc_col[:, h:h + 1])
            erests.append(erest_col[:, h:h + 1])
            elasts.append(elast[:, h:h + 1])
    groups = range(nb * heads)

    qs = [q * lax.rsqrt(_row_sums(q * q) + EPS) * (hd ** -0.5) for q in qs]
    ks = [k * lax.rsqrt(_row_sums(k * k) + EPS) for k in ks]
    kbs = [k * b for k, b in zip(ks, betas)]
    k16 = [k.astype(BF16) for k in ks]
    decays = [jnp.where(incl, jnp.exp(jnp.where(incl, gcol - grow, 0.0)), 0.0) for gcol, grow in zip(gcols, grows)]
    lmats = [lax.dot_general(kb.astype(BF16), k, NT_DIMS, preferred_element_type=F32) * jnp.where(strict, dec, 0.0)
             for kb, k, dec in zip(kbs, k16, decays)]
    attns = [(lax.dot_general(q.astype(BF16), k, NT_DIMS, preferred_element_type=F32) * dec).astype(BF16)
             for q, k, dec in zip(qs, k16, decays)]

    invs = [eye - l for l in lmats]
    pw16 = [l.astype(BF16) for l in lmats]
    span = 1
    while span < c // 2:
        pw16 = [_dot(p, p).astype(BF16) for p in pw16]
        invs = [x + _dot(x.astype(BF16), p) for x, p in zip(invs, pw16)]
        span *= 2
    rhs = [jnp.concatenate([v * b, kb * e], axis=1).astype(BF16) for v, b, kb, e in zip(vs, betas, kbs, egcs)]
    sols = [_dot(x.astype(BF16), r) for x, r in zip(invs, rhs)]

    states = [state_ref[g] for g in groups]
    s16 = [s.astype(BF16) for s in states]
    vn16 = [(sol[:, :hd] - _dot(sol[:, hd:].astype(BF16), s)).astype(BF16) for sol, s in zip(sols, s16)]
    outs = [_dot((q * e).astype(BF16), s) + _dot(a, vn) for q, e, s, a, vn in zip(qs, egcs, s16, attns, vn16)]
    for g in groups:
        k_dec = (ks[g] * erests[g]).astype(BF16)
        state_ref[g] = states[g] * elasts[g] + lax.dot_general(k_dec, vn16[g], TN_DIMS, preferred_element_type=F32)

    outs = [o * lax.rsqrt(_row_sums(o * o) * (1.0 / hd) + EPS) * gain_ref[...] for o in outs]
    for g in groups:
        bi, h = divmod(g, heads)
        zh = z_ref[bi, :, h * hd:(h + 1) * hd].astype(F32)
        o_ref[bi, :, h * hd:(h + 1) * hd] = (outs[g] * (zh * _sigmoid(zh))).astype(BF16)


def _deltanet(proj3, ab3, abt3, conv_w, prow, pcol, gain, *, heads, qkv_block, z_block, chunk=128, nb=4):
    b, s, _ = proj3.shape
    width = heads * HEAD_DIM
    assert s % chunk == 0 and b % nb == 0
    vmem = 4 * nb * chunk * 5 * width * 2 + nb * heads * 16 * chunk * HEAD_DIM * 4 + (8 << 20)
    return pl.pallas_call(
        functools.partial(_dn_kernel, heads=heads),
        grid=(b // nb, s // chunk),
        in_specs=[
            pl.BlockSpec((nb, chunk, 3 * width), lambda i, n: (i, n, qkv_block)),
            pl.BlockSpec((nb, chunk, width), lambda i, n: (i, n, z_block)),
            pl.BlockSpec((nb, chunk, LANES), lambda i, n: (i, n, 0)),
            pl.BlockSpec((nb, SUBLANES, chunk), lambda i, n: (i, 0, n)),
            _resident((DN_CONV, 3 * width)),
            _resident((2, LANES)),
            _resident((SUBLANES, 2)),
            _resident((1, HEAD_DIM)),
        ],
        out_specs=pl.BlockSpec((nb, chunk, width), lambda i, n: (i, n, 0)),
        out_shape=jax.ShapeDtypeStruct((b, s, width), BF16),
        scratch_shapes=[
            pltpu.VMEM((nb * heads, HEAD_DIM, HEAD_DIM), F32),
            pltpu.VMEM((nb, SUBLANES, 3 * width), F32),
        ],
        compiler_params=_cparams(("parallel", "arbitrary"), vmem),
        name="deltanet",
    )(proj3, proj3, ab3, abt3, conv_w, prow, pcol, gain)


SB_UNDERFLOW = 105.0


def _sb_blocks(qs, ks, vs, runs, accs, upper, causal):
    zs = [lax.dot_general(q, k, NT_DIMS, preferred_element_type=F32) for q, k in zip(qs, ks)]
    sps = [_softplus(z) for z in zs]
    masked = sps if causal is None else [jnp.where(causal, sp, 0.0) for sp in sps]
    tails = [_dot(m.astype(BF16), upper) + run for m, run in zip(masked, runs)]
    wgts = [jnp.exp(z - sp - tail) for z, sp, tail in zip(zs, sps, tails)]
    if causal is not None:
        wgts = [jnp.where(causal, w, 0.0) for w in wgts]
    accs = [acc + _dot(w.astype(BF16), v) for acc, w, v in zip(accs, wgts, vs)]
    runs = [run + jnp.sum(m, axis=-1, keepdims=True) for run, m in zip(runs, masked)]
    return runs, accs


def _sb_kernel(q_ref, k_ref, v_ref, o_ref, *, heads, tk):
    tq = q_ref.shape[0]
    hd = HEAD_DIM
    assert tq == tk
    qi = pl.program_id(1)
    ri = lax.broadcasted_iota(jnp.int32, (tk, tk), 0)
    ci = lax.broadcasted_iota(jnp.int32, (tk, tk), 1)
    upper = jnp.where(ri > ci, 1.0, 0.0).astype(BF16)
    qs = [q_ref[:, h * hd:(h + 1) * hd] for h in range(heads)]

    def block(blk, runs, accs, causal):
        r0 = pl.multiple_of(blk * tk, tk)
        ks = [k_ref[pl.ds(r0, tk), h * hd:(h + 1) * hd] for h in range(heads)]
        vs = [v_ref[pl.ds(r0, tk), h * hd:(h + 1) * hd] for h in range(heads)]
        return _sb_blocks(qs, ks, vs, runs, accs, upper, causal)

    runs = [jnp.zeros((tq, 1), F32)] * heads
    accs = [jnp.zeros((tq, hd), F32)] * heads
    runs, accs = block(qi, runs, accs, ci < ri)

    def least(runs):
        return jnp.min(functools.reduce(jnp.minimum, runs))

    def more(carry):
        return (carry[0] < qi) & (carry[1] < SB_UNDERFLOW)

    def body(carry):
        i, _, runs, accs = carry
        runs, accs = block(qi - 1 - i, list(runs), list(accs), None)
        return i + 1, least(runs), tuple(runs), tuple(accs)

    _, _, _, accs = lax.while_loop(more, body, (jnp.int32(0), least(runs), tuple(runs), tuple(accs)))
    for h in range(heads):
        o_ref[:, h * hd:(h + 1) * hd] = accs[h].astype(BF16)


def _stickbreak(proj3, *, heads, q_block, tq=256):
    b, s, _ = proj3.shape
    width = heads * HEAD_DIM
    assert s % tq == 0 and q_block % heads == 0
    vmem = 4 * s * width * 2 + heads * 24 * tq * tq * 4 + (8 << 20)
    return pl.pallas_call(
        functools.partial(_sb_kernel, heads=heads, tk=tq),
        grid=(b, s // tq),
        in_specs=[
            pl.BlockSpec((None, tq, width), lambda i, t: (i, t, q_block // heads)),
            pl.BlockSpec((None, s, width), lambda i, t: (i, 0, q_block // heads + 1)),
            pl.BlockSpec((None, s, width), lambda i, t: (i, 0, q_block // heads + 2)),
        ],
        out_specs=pl.BlockSpec((None, tq, width), lambda i, t: (i, t, 0)),
        out_shape=jax.ShapeDtypeStruct((b, s, width), BF16),
        compiler_params=_cparams(("parallel", "arbitrary"), vmem),
        name="stickbreak",
    )(proj3, proj3, proj3)


def _merge_ffn_kernel(x_ref, yp_ref, yd_ref, ys_ref, gl_ref, bg_ref, wb_ref, wo_ref, gain_ref, wg_ref, wu_ref, wd_ref,
                      fgain_ref, o_ref, act_ref, *, f_chunk, final_norm):
    d = x_ref.shape[1]
    merged = None
    for i, y_ref in enumerate((yp_ref, yd_ref, ys_ref)):
        sl = slice(i * d, (i + 1) * d)
        gate = _sigmoid(gl_ref[:, sl].astype(F32) + bg_ref[:, sl])
        term = gate * _dot(y_ref[...], wb_ref[i])
        merged = term if merged is None else merged + term
    x = x_ref[...] + _dot(merged.astype(BF16), wo_ref[...])
    o_ref[...] = _ffn_half_step(x, gain_ref, wg_ref, wu_ref, wd_ref, fgain_ref, act_ref, f_chunk, final_norm)


def _merge_ffn(x2d, y_pool, y_dn, y_sb, proj2, b_gate, w_branch, w_out, gain, wg, wu, wd, final_gain, *,
               which, final_norm, tm=512, f_chunk=256):
    t, d = x2d.shape
    nb, bw, _ = w_branch.shape
    d_ff = wg.shape[-1]
    assert t % tm == 0 and d_ff % f_chunk == 0
    vmem = (4 * tm * d * 4 + 6 * tm * bw * 2 + 2 * tm * nb * d * 2 + nb * bw * d * 2 + d * d * 2 + 3 * d * d_ff * 2
            + tm * d_ff * 2 + 6 * tm * d * 4 + (6 << 20))
    y_spec = pl.BlockSpec((tm, bw), lambda i: (i, 0))
    return pl.pallas_call(
        functools.partial(_merge_ffn_kernel, f_chunk=f_chunk, final_norm=final_norm),
        grid=(t // tm,),
        in_specs=[
            pl.BlockSpec((tm, d), lambda i: (i, 0)),
            y_spec, y_spec, y_spec,
            pl.BlockSpec((tm, nb * d), lambda i: (i, 0)),
            _resident((1, nb * d)),
            _resident((nb, bw, d)),
            _resident((d, d)),
            _resident((1, d)),
            _resident_slice(wg.shape, which),
            _resident_slice(wu.shape, which),
            _resident_slice(wd.shape, which),
            _resident((1, d)),
        ],
        out_specs=pl.BlockSpec((tm, d), lambda i: (i, 0)),
        out_shape=jax.ShapeDtypeStruct((t, d), F32),
        scratch_shapes=[pltpu.VMEM((tm, d_ff), BF16)],
        compiler_params=_cparams(("parallel",), vmem),
        name="merge_ffn",
    )(x2d, y_pool, y_dn, y_sb, proj2, b_gate, w_branch, w_out, gain, wg, wu, wd, final_gain)


def kernel(x, ffn_norm, ffn_w_gate, ffn_w_up, ffn_w_down, mix_norm, w_in, b_gate, pool_w, pool_scale, dn_conv,
           dn_A_log, dn_dt_bias, dn_out_norm, w_branch, w_out, final_norm):
    bsz, seq, d = x.shape
    depth = w_in.shape[0]
    n_branch, bw = w_branch.shape[1], w_branch.shape[2]
    heads = bw // HEAD_DIM
    t = bsz * seq

    splits = (bw, 3 * bw, bw, heads, heads, 3 * bw, n_branch * d)
    offs = [0]
    for w in splits:
        offs.append(offs[-1] + w)
    o_pool, o_dnqkv, o_z, o_a, o_b, o_sb, o_gate, o_end = offs
    assert o_end == w_in.shape[2]
    c_gate, c_dnqkv = 0, n_branch * d
    c_sb = c_dnqkv + 3 * bw
    c_pool = c_sb + 3 * bw
    c_z = c_pool + bw
    n_main = c_z + bw

    wg16, wu16, wd16 = ffn_w_gate.astype(BF16), ffn_w_up.astype(BF16), ffn_w_down.astype(BF16)
    x2 = x.reshape(t, d)
    for l in range(depth):
        wl = w_in[l]
        w_main = jnp.concatenate(
            [wl[:, o_gate:o_end], wl[:, o_dnqkv:o_z], wl[:, o_sb:o_gate], wl[:, o_pool:o_dnqkv], wl[:, o_z:o_a]],
            axis=1).astype(BF16)
        w_ab = jnp.pad(wl[:, o_a:o_sb], ((0, 0), (0, LANES - 2 * heads))).astype(BF16)
        row = lambda v: v.reshape(1, -1)

        x2 = _ffn(x2, row(ffn_norm[l, 0]), wg16, wu16, wd16, row(final_norm), which=(l, 0), final_norm=False)

        proj, ab = _proj(x2, row(mix_norm[l]), w_main, w_ab,
                         q_cols=(c_sb, c_sb + bw), q_scale=HEAD_DIM ** -0.5)
        proj3 = proj.reshape(bsz, seq, n_main)
        ab3 = ab.reshape(bsz, seq, LANES)
        abt3 = jnp.swapaxes(ab3[:, :, :SUBLANES], 1, 2)

        y_pool = _pool(proj3, pool_w[l].astype(BF16), row(pool_scale[l]), col_block=c_pool // bw)

        pad_lanes = LANES - heads
        prow = jnp.stack([jnp.pad(dn_A_log[l], (0, pad_lanes)), jnp.pad(dn_dt_bias[l], (0, pad_lanes))])
        pcol = jnp.pad(jnp.stack([dn_A_log[l], dn_dt_bias[l]], axis=1), ((0, SUBLANES - heads), (0, 0)))
        y_dn = _deltanet(proj3, ab3, abt3, dn_conv[l], prow, pcol, row(dn_out_norm[l]), heads=heads,
                         qkv_block=c_dnqkv // (3 * bw), z_block=c_z // bw)

        y_sb = _stickbreak(proj3, heads=heads, q_block=c_sb // HEAD_DIM)

        x2 = _merge_ffn(x2, y_pool.reshape(t, bw), y_dn.reshape(t, bw), y_sb.reshape(t, bw), proj,
                        row(b_gate[l]), w_branch[l].astype(BF16), w_out[l].astype(BF16),
                        row(ffn_norm[l, 1]), wg16, wu16, wd16, row(final_norm), which=(l, 1),
                        final_norm=(l == depth - 1))
    return x2.reshape(bsz, seq, d)
```
